```python
import jax, jax.numpy as jnp
from jax import lax
import numpy as np

D_MODEL = 2048
BATCH = 4
SEQ = 2048
DEPTH = 2
DEC_BATCH = 128
DEC_SEQ = 8
PAST_LEN = 8192
PAGE_SIZE = 128

N_BRANCH = 4
BRANCH_W = D_MODEL // 2
A_HD = 64
A_HEADS = BRANCH_W // A_HD
A_DECAY_LORA = 64
A_AAA_LORA = 64
B_HD = 64
B_HEADS = BRANCH_W // B_HD
B_KV_HEADS = B_HEADS // 8
WINDOW = 128
ROPE_THETA = 10000.0
C_HEADS = 8
C_V = BRANCH_W // C_HEADS
C_QK = C_V // 2
C_CONV = 4
D_HEADS = 8
D_QK = BRANCH_W // D_HEADS
D_V = BRANCH_W // D_HEADS
CHUNK = 64
LN_EPS = 1e-5
NEG_INF = -1e30
ALPHA = (2 * DEPTH) ** 0.25
BETA = (8 * DEPTH) ** -0.25

A_COLS = 3 * BRANCH_W + A_DECAY_LORA + A_AAA_LORA
B_COLS = (B_HEADS + 2 * B_KV_HEADS) * B_HD
C_CONV_COLS = 2 * C_HEADS * C_QK
C_COLS = C_CONV_COLS + C_HEADS * C_V + 2 * C_HEADS
D_COLS = D_HEADS * (2 * D_QK + D_V)
MIX_COLS = A_COLS + B_COLS + C_COLS + D_COLS
IN_COLS = MIX_COLS + N_BRANCH * BRANCH_W + N_BRANCH * D_MODEL

kernel_name = 'hybrid_rwkv7_swa_mlstm_retention_step'


def _layer_norm(x, g, b):
    xf = x.astype(jnp.float32)
    mu = jnp.mean(xf, -1, keepdims=True)
    var = jnp.mean(jnp.square(xf - mu), -1, keepdims=True)
    return ((xf - mu) * lax.rsqrt(var + LN_EPS) * g + b).astype(x.dtype)


def _head_norm(y, g, b):
    mu = jnp.mean(y, -1, keepdims=True)
    var = jnp.mean(jnp.square(y - mu), -1, keepdims=True)
    yn = (y - mu) * lax.rsqrt(var + LN_EPS)
    return yn.reshape(y.shape[0], y.shape[1], -1) * g + b


def _rope(x, pos):
    d = x.shape[-1]
    inv = ROPE_THETA ** (-jnp.arange(0, d, 2, dtype=jnp.float32) / d)
    ang = pos.astype(jnp.float32)[:, None] * inv[None, :]
    cos = jnp.cos(ang)[None, :, None, :]
    sin = jnp.sin(ang)[None, :, None, :]
    xf = x.astype(jnp.float32)
    x1, x2 = xf[..., : d // 2], xf[..., d // 2:]
    return jnp.concatenate([x1 * cos - x2 * sin, x1 * sin + x2 * cos], -1).astype(x.dtype)


def _chunk_len(L):
    return CHUNK if L % CHUNK == 0 else L


def _to_chunks(t, Lc):
    Bn, L = t.shape[:2]
    return jnp.moveaxis(t.reshape((Bn, L // Lc, Lc) + t.shape[2:]), 1, 0)


def _from_chunks(t):
    nc, Bn, Lc = t.shape[:3]
    return jnp.moveaxis(t, 0, 1).reshape((Bn, nc * Lc) + t.shape[3:])


def _rwkv7_branch(pa, shift_prev, S0, mu, w0, w_up, a0, a_up, k_k, k_a, r_k, gn_g, gn_b):
    Bn, L, _ = pa.shape
    W = BRANCH_W
    prev = jnp.concatenate([shift_prev.astype(pa.dtype)[:, None, :], pa[:, :-1]], axis=1)
    xs = (pa + (prev - pa) * mu).astype(jnp.float32)
    r = xs[..., :W]
    k = xs[..., W:2 * W]
    v = xs[..., 2 * W:3 * W]
    w_lo = xs[..., 3 * W:3 * W + A_DECAY_LORA]
    a_lo = xs[..., 3 * W + A_DECAY_LORA:]
    w_log = -jax.nn.softplus(-(w0 + jnp.tanh(w_lo) @ w_up)) - 0.5
    decay = jnp.exp(-jnp.exp(w_log))
    a = jax.nn.sigmoid(a0 + a_lo @ a_up)
    heads = lambda t: t.reshape(Bn, L, A_HEADS, A_HD)
    kk = heads(k * k_k)
    kk = kk / jnp.maximum(jnp.sqrt(jnp.sum(jnp.square(kk), -1, keepdims=True)), 1e-12)
    k = k * (1.0 + (a - 1.0) * k_a)
    r, k, v, decay, a = heads(r), heads(k), heads(v), heads(decay), heads(a)

    def step(S, inp):
        r_t, w_t, k_t, v_t, kk_t, a_t = inp
        sa = jnp.einsum('bhvk,bhk->bhv', S, -kk_t)
        S = (S * w_t[:, :, None, :] + sa[..., None] * (kk_t * a_t)[:, :, None, :]
             + v_t[..., None] * k_t[:, :, None, :])
        return S, jnp.einsum('bhvk,bhk->bhv', S, r_t)

    seq = tuple(jnp.moveaxis(t, 1, 0) for t in (r, decay, k, v, kk, a))
    S, y = lax.scan(step, S0.astype(jnp.float32), seq)
    y = jnp.moveaxis(y, 0, 1)
    bonus = jnp.sum(r * k * r_k, -1, keepdims=True) * v
    y = _head_norm(y, gn_g, gn_b) + bonus.reshape(Bn, L, W)
    return y, S.astype(S0.dtype), pa[:, -1].astype(shift_prev.dtype)


def _swa_branch(pb, pos, pos0, k_buf, v_buf, sinks):
    Bn, L, _ = pb.shape
    G = B_HEADS // B_KV_HEADS
    qw = B_HEADS * B_HD
    kw = B_KV_HEADS * B_HD
    q = _rope(pb[..., :qw].reshape(Bn, L, B_HEADS, B_HD), pos)
    k = _rope(pb[..., qw:qw + kw].reshape(Bn, L, B_KV_HEADS, B_HD), pos)
    v = pb[..., qw + kw:].reshape(Bn, L, B_KV_HEADS, B_HD)
    k_buf = k_buf.astype(k.dtype)
    v_buf = v_buf.astype(v.dtype)
    k_ext = jnp.concatenate([k_buf, k], 1)
    v_ext = jnp.concatenate([v_buf, v], 1)
    Lb = WINDOW if L % WINDOW == 0 else L
    nb = L // Lb
    if Lb == WINDOW:
        prev_k = k_ext[:, :L].reshape(Bn, nb, WINDOW, B_KV_HEADS, B_HD)
        prev_v = v_ext[:, :L].reshape(Bn, nb, WINDOW, B_KV_HEADS, B_HD)
    else:
        prev_k = k_buf[:, None]
        prev_v = v_buf[:, None]
    kb = jnp.concatenate([prev_k, k.reshape(Bn, nb, Lb, B_KV_HEADS, B_HD)], 2)
    vb = jnp.concatenate([prev_v, v.reshape(Bn, nb, Lb, B_KV_HEADS, B_HD)], 2)
    qb = q.reshape(Bn, nb, Lb, B_KV_HEADS, G, B_HD)
    a_idx = jnp.arange(Lb)[:, None]
    c_idx = jnp.arange(WINDOW + Lb)[None, :]
    key_pos = pos0 + jnp.arange(nb)[:, None, None] * Lb - WINDOW + c_idx
    mask = (c_idx >= a_idx) & (c_idx <= WINDOW + a_idx) & (key_pos >= 0)
    s = jnp.einsum('bnqhgd,bnkhd->bnhgqk', qb, kb).astype(jnp.float32) * (B_HD ** -0.5)
    s = jnp.where(mask[None, :, None, None], s, NEG_INF)
    sink = sinks.astype(jnp.float32).reshape(B_KV_HEADS, G)[:, :, None, None]
    m = jnp.maximum(jnp.max(s, -1, keepdims=True), sink)
    p = jnp.exp(s - m)
    p = p / (jnp.sum(p, -1, keepdims=True) + jnp.exp(sink - m))
    o = jnp.einsum('bnhgqk,bnkhd->bnqhgd', p.astype(vb.dtype), vb).reshape(Bn, L, BRANCH_W)
    return o, k_ext[:, -WINDOW:], v_ext[:, -WINDOW:]


def _mlstm_chunkwise(q, k, v, logi, logf, C0, n0, m0):
    L = q.shape[1]
    Lc = _chunk_len(L)
    causal = jnp.tril(jnp.ones((Lc, Lc), dtype=bool))

    def step(carry, inp):
        C, n, m = carry
        qc, kc, vc, li, lf = inp
        b = jnp.moveaxis(jnp.cumsum(lf, axis=1), 1, -1)
        li = jnp.moveaxis(li, 1, -1)
        dmat = jnp.where(causal, b[..., :, None] - b[..., None, :] + li[..., None, :], NEG_INF)
        inter = b + m[..., None]
        mt = jnp.maximum(inter, jnp.max(dmat, -1))
        wts = jnp.exp(dmat - mt[..., None])
        sc = jnp.exp(inter - mt)
        A = jnp.einsum('bqhd,bkhd->bhqk', qc, kc) * wts
        num = jnp.einsum('bhqk,bkhv->bhqv', A, vc) + sc[..., None] * jnp.einsum('bqhd,bhdv->bhqv', qc, C)
        den = jnp.sum(A, -1) + sc * jnp.einsum('bqhd,bhd->bhq', qc, n)
        h = num / jnp.maximum(jnp.abs(den), jnp.exp(-mt))[..., None]
        gl = b[..., -1:] - b + li
        m_new = jnp.maximum(b[..., -1] + m, jnp.max(gl, -1))
        wj = jnp.exp(gl - m_new[..., None])
        s_old = jnp.exp(b[..., -1] + m - m_new)
        C_new = s_old[..., None, None] * C + jnp.einsum('bhk,bkhd,bkhv->bhdv', wj, kc, vc)
        n_new = s_old[..., None] * n + jnp.einsum('bhk,bkhd->bhd', wj, kc)
        return (C_new, n_new, m_new), jnp.moveaxis(h, 1, 2)

    (C, n, m), hs = lax.scan(step, (C0, n0, m0), tuple(_to_chunks(t, Lc) for t in (q, k, v, logi, logf)))
    return _from_chunks(hs), C, n, m


def _mlstm_branch(pc, conv_buf, C0, n0, m0, conv_w, conv_b, i_bias, f_bias, gn_g, gn_b):
    Bn, L, _ = pc.shape
    u = pc[..., :C_CONV_COLS]
    ext = jnp.concatenate([conv_buf.astype(u.dtype), u], 1)
    conv = conv_b + ext[:, 0:L] * conv_w[0]
    for j in range(1, C_CONV):
        conv = conv + ext[:, j:j + L] * conv_w[j]
    conv = jax.nn.silu(conv.astype(jnp.float32))
    qk_w = C_HEADS * C_QK
    q = conv[..., :qk_w].reshape(Bn, L, C_HEADS, C_QK)
    k = conv[..., qk_w:].reshape(Bn, L, C_HEADS, C_QK) * (C_QK ** -0.5)
    v = pc[..., C_CONV_COLS:C_CONV_COLS + C_HEADS * C_V].astype(jnp.float32).reshape(Bn, L, C_HEADS, C_V)
    gates = pc[..., C_CONV_COLS + C_HEADS * C_V:].astype(jnp.float32)
    logi = gates[..., :C_HEADS] + i_bias
    logf = jax.nn.log_sigmoid(gates[..., C_HEADS:] + f_bias)
    h, C, n, m = _mlstm_chunkwise(q, k, v, logi, logf, C0.astype(jnp.float32),
                                  n0.astype(jnp.float32), m0.astype(jnp.float32))
    y = _head_norm(h, gn_g, gn_b)
    return (y, C.astype(C0.dtype), n.astype(n0.dtype), m.astype(m0.dtype),
            ext[:, -(C_CONV - 1):].astype(conv_buf.dtype))


def _retention_branch(pd, pos, S0, gn_g, gn_b):
    Bn, L, _ = pd.shape
    qw = D_HEADS * D_QK
    q = _rope(pd[..., :qw].reshape(Bn, L, D_HEADS, D_QK), pos).astype(jnp.float32)
    k = _rope(pd[..., qw:2 * qw].reshape(Bn, L, D_HEADS, D_QK), pos).astype(jnp.float32) * (D_QK ** -0.5)
    v = pd[..., 2 * qw:].astype(jnp.float32).reshape(Bn, L, D_HEADS, D_V)
    log_gamma = jnp.log1p(-jnp.exp2(-5.0 - jnp.arange(D_HEADS, dtype=jnp.float32)))
    Lc = _chunk_len(L)
    idx = jnp.arange(Lc, dtype=jnp.float32)
    rel = idx[:, None] - idx[None, :]
    decay_mat = jnp.where(rel >= 0, jnp.exp(jnp.maximum(rel, 0.0) * log_gamma[:, None, None]), 0.0)
    q_dec = jnp.exp((idx + 1.0)[None, :] * log_gamma[:, None])
    k_dec = jnp.exp((Lc - 1.0 - idx)[None, :] * log_gamma[:, None])
    c_dec = jnp.exp(Lc * log_gamma)

    def step(S, inp):
        qc, kc, vc = inp
        inner = jnp.einsum('bqhd,bkhd->bhqk', qc, kc) * decay_mat
        o = (jnp.einsum('bhqk,bkhv->bqhv', inner, vc)
             + jnp.einsum('bqhd,bhdv->bqhv', qc, S) * q_dec.T[None, :, :, None])
        S = c_dec[None, :, None, None] * S + jnp.einsum('bkhd,bkhv,hk->bhdv', kc, vc, k_dec)
        return S, o

    S, os_ = lax.scan(step, S0.astype(jnp.float32), tuple(_to_chunks(t, Lc) for t in (q, k, v)))
    return _head_norm(_from_chunks(os_), gn_g, gn_b), S.astype(S0.dtype)


def _hybrid_layer(x, pos0, S_a, shift_a, k_buf, v_buf, C_m, n_m, m_m, conv_m, S_d,
                  w_in, a_mu, a_w0, a_w_up, a_a0, a_a_up, a_k_k, a_k_a, a_r_k, a_ln_g, a_ln_b,
                  b_sinks, c_conv_w, c_conv_b, c_i_bias, c_f_bias, c_ln_g, c_ln_b,
                  d_ln_g, d_ln_b, w_branch, w_out, ln_g, ln_b):
    Bn, L, _ = x.shape
    pos = pos0 + jnp.arange(L)
    proj = x @ w_in
    o1 = A_COLS
    o2 = o1 + B_COLS
    o3 = o2 + C_COLS
    o4 = MIX_COLS
    o5 = o4 + N_BRANCH * BRANCH_W
    y_a, S_a, shift_a = _rwkv7_branch(proj[..., :o1], shift_a, S_a, a_mu, a_w0, a_w_up, a_a0, a_a_up,
                                      a_k_k, a_k_a, a_r_k, a_ln_g, a_ln_b)
    y_b, k_buf, v_buf = _swa_branch(proj[..., o1:o2], pos, pos0, k_buf, v_buf, b_sinks)
    y_c, C_m, n_m, m_m, conv_m = _mlstm_branch(proj[..., o2:o3], conv_m, C_m, n_m, m_m, c_conv_w, c_conv_b,
                                               c_i_bias, c_f_bias, c_ln_g, c_ln_b)
    y_d, S_d = _retention_branch(proj[..., o3:o4], pos, S_d, d_ln_g, d_ln_b)
    z = jax.nn.silu(proj[..., o4:o5].astype(jnp.float32)).reshape(Bn, L, N_BRANCH, BRANCH_W)
    gate = jax.nn.sigmoid(proj[..., o5:].astype(jnp.float32)).reshape(Bn, L, N_BRANCH, D_MODEL)
    merged = jnp.zeros((Bn, L, D_MODEL), jnp.float32)
    for i, y_i in enumerate((y_a, y_b, y_c, y_d)):
        branch = (y_i * z[:, :, i]).astype(x.dtype) @ w_branch[i]
        merged = merged + gate[:, :, i] * branch
    out = merged.astype(x.dtype) @ w_out
    x_new = _layer_norm(ALPHA * x + out, ln_g, ln_b)
    return x_new, (S_a, shift_a, k_buf, v_buf, C_m, n_m, m_m, conv_m, S_d)


def _trunk(x, pos0, states, weights):
    new = []
    for l in range(DEPTH):
        x, st = _hybrid_layer(x, pos0, *[s[l] for s in states], *[w[l] for w in weights])
        new.append(st)
    stacked = tuple(jnp.stack([st[j] for st in new]) for j in range(len(states)))
    return x, stacked


def setup_inputs(seed: int = 0) -> dict:
    key = jax.random.key(seed)
    ks = iter(jax.random.split(key, 48))
    f32 = jnp.float32
    nrm = lambda shape, scale: scale * jax.random.normal(next(ks), shape, f32)
    W = BRANCH_W
    inp = {}
    inp['x_prompt'] = nrm((BATCH, SEQ, D_MODEL), 1.0)
    inp['x_sample'] = nrm((DEC_BATCH, DEC_SEQ, D_MODEL), 1.0)
    inp['state_rwkv_S'] = nrm((DEPTH, DEC_BATCH, A_HEADS, A_HD, A_HD), 0.3)
    inp['state_rwkv_shift'] = nrm((DEPTH, DEC_BATCH, A_COLS), 1.0)
    inp['cache_swa_k'] = nrm((DEPTH, DEC_BATCH, WINDOW, B_KV_HEADS, B_HD), 1.0)
    inp['cache_swa_v'] = nrm((DEPTH, DEC_BATCH, WINDOW, B_KV_HEADS, B_HD), 1.0)
    inp['state_mlstm_C'] = nrm((DEPTH, DEC_BATCH, C_HEADS, C_QK, C_V), 0.3)
    inp['state_mlstm_n'] = nrm((DEPTH, DEC_BATCH, C_HEADS, C_QK), 0.3)
    inp['state_mlstm_m'] = nrm((DEPTH, DEC_BATCH, C_HEADS), 0.5)
    inp['state_mlstm_conv'] = nrm((DEPTH, DEC_BATCH, C_CONV - 1, C_CONV_COLS), 1.0)
    inp['state_ret_S'] = nrm((DEPTH, DEC_BATCH, D_HEADS, D_QK, D_V), 0.3)
    inp['w_in'] = nrm((DEPTH, D_MODEL, IN_COLS), D_MODEL ** -0.5)
    inp['a_mu'] = jax.random.uniform(next(ks), (DEPTH, A_COLS), f32)
    inp['a_w0'] = nrm((DEPTH, W), 1.0) - 1.0
    inp['a_w_up'] = nrm((DEPTH, A_DECAY_LORA, W), 0.1 * A_DECAY_LORA ** -0.5)
    inp['a_a0'] = nrm((DEPTH, W), 0.1)
    inp['a_a_up'] = nrm((DEPTH, A_AAA_LORA, W), A_AAA_LORA ** -0.5)
    inp['a_k_k'] = 0.85 + nrm((DEPTH, W), 0.05)
    inp['a_k_a'] = 1.0 + nrm((DEPTH, W), 0.05)
    inp['a_r_k'] = nrm((DEPTH, A_HEADS, A_HD), 0.1)
    inp['a_ln_g'] = 1.0 + nrm((DEPTH, W), 0.05)
    inp['a_ln_b'] = nrm((DEPTH, W), 0.02)
    inp['b_sinks'] = nrm((DEPTH, B_HEADS), 0.5)
    inp['c_conv_w'] = nrm((DEPTH, C_CONV, C_CONV_COLS), C_CONV ** -0.5)
    inp['c_conv_b'] = nrm((DEPTH, C_CONV_COLS), 0.02)
    inp['c_i_bias'] = nrm((DEPTH, C_HEADS), 0.5) - 2.0
    inp['c_f_bias'] = jnp.linspace(3.0, 6.0, C_HEADS, dtype=f32)[None, :] + nrm((DEPTH, C_HEADS), 0.1)
    inp['c_ln_g'] = 1.0 + nrm((DEPTH, W), 0.05)
    inp['c_ln_b'] = nrm((DEPTH, W), 0.02)
    inp['d_ln_g'] = 1.0 + nrm((DEPTH, W), 0.05)
    inp['d_ln_b'] = nrm((DEPTH, W), 0.02)
    inp['w_branch'] = nrm((DEPTH, N_BRANCH, W, D_MODEL), BETA * W ** -0.5)
    inp['w_out'] = nrm((DEPTH, D_MODEL, D_MODEL), BETA * D_MODEL ** -0.5)
    inp['ln_g'] = 1.0 + nrm((DEPTH, D_MODEL), 0.05)
    inp['ln_b'] = nrm((DEPTH, D_MODEL), 0.02)
    return inp


def reference(x_prompt, x_sample, state_rwkv_S, state_rwkv_shift, cache_swa_k, cache_swa_v,
              state_mlstm_C, state_mlstm_n, state_mlstm_m, state_mlstm_conv, state_ret_S,
              w_in, a_mu, a_w0, a_w_up, a_a0, a_a_up, a_k_k, a_k_a, a_r_k, a_ln_g, a_ln_b,
              b_sinks, c_conv_w, c_conv_b, c_i_bias, c_f_bias, c_ln_g, c_ln_b,
              d_ln_g, d_ln_b, w_branch, w_out, ln_g, ln_b):
    weights = (w_in, a_mu, a_w0, a_w_up, a_a0, a_a_up, a_k_k, a_k_a, a_r_k, a_ln_g, a_ln_b,
               b_sinks, c_conv_w, c_conv_b, c_i_bias, c_f_bias, c_ln_g, c_ln_b,
               d_ln_g, d_ln_b, w_branch, w_out, ln_g, ln_b)
    sample_states = (state_rwkv_S, state_rwkv_shift, cache_swa_k, cache_swa_v,
                     state_mlstm_C, state_mlstm_n, state_mlstm_m, state_mlstm_conv, state_ret_S)
    n_prompt = x_prompt.shape[0]
    zero_states = tuple(jnp.zeros((DEPTH, n_prompt) + s.shape[2:], s.dtype) for s in sample_states)
    y_prompt, p = _trunk(x_prompt, 0, zero_states, weights)
    y_sample, s = _trunk(x_sample, PAST_LEN, sample_states, weights)
    return (y_prompt, y_sample, p[0], s[0], p[1], s[1], p[2], s[2], p[3], s[3], p[4], s[4],
            p[5], s[5], p[6], s[6], p[7], s[7], p[8], s[8])
```

```python
import functools

import jax
import jax.numpy as jnp
from jax import lax
from jax.experimental import pallas as pl
from jax.experimental.pallas import tpu as pltpu

F32 = jnp.float32
BF16 = jnp.bfloat16

D_MODEL = 2048
DEPTH = 2
PAST_LEN = 8192
N_BRANCH = 4
BRANCH_W = D_MODEL // 2
A_HD = 64
A_HEADS = BRANCH_W // A_HD
A_DECAY_LORA = 64
A_AAA_LORA = 64
B_HD = 64
B_HEADS = BRANCH_W // B_HD
B_KV_HEADS = B_HEADS // 8
WINDOW = 128
ROPE_THETA = 10000.0
C_HEADS = 8
C_V = BRANCH_W // C_HEADS
C_QK = C_V // 2
C_CONV = 4
D_HEADS = 8
D_QK = BRANCH_W // D_HEADS
D_V = BRANCH_W // D_HEADS
CHUNK = 64
LN_EPS = 1e-5
NEG_INF = -1e30
ALPHA = (2 * DEPTH) ** 0.25

A_COLS = 3 * BRANCH_W + A_DECAY_LORA + A_AAA_LORA
B_COLS = (B_HEADS + 2 * B_KV_HEADS) * B_HD
C_CONV_COLS = 2 * C_HEADS * C_QK
C_COLS = C_CONV_COLS + C_HEADS * C_V + 2 * C_HEADS
D_COLS = D_HEADS * (2 * D_QK + D_V)
MIX_COLS = A_COLS + B_COLS + C_COLS + D_COLS

LANES = 128
SUBLANES = 8
VMEM_LIMIT_BYTES = 56 * 1024 * 1024

NT_DIMS = (((1,), (1,)), ((), ()))
TN_DIMS = (((0,), (0,)), ((), ()))


def _params(*sem):
    return pltpu.CompilerParams(dimension_semantics=sem, vmem_limit_bytes=VMEM_LIMIT_BYTES)


def _full(shape):
    n = len(shape)
    return pl.BlockSpec(shape, lambda *_: (0,) * n)


def _softplus(x):
    return jnp.maximum(x, 0.0) + jnp.log1p(jnp.exp(-jnp.abs(x)))


def _split2(x):
    hi = x.astype(BF16)
    lo = (x - hi.astype(F32)).astype(BF16)
    return hi, lo


def _pair_ones():
    r = lax.broadcasted_iota(jnp.int32, (LANES, LANES), 0) // A_HD
    c = lax.broadcasted_iota(jnp.int32, (LANES, LANES), 1) // A_HD
    return jnp.where(r == c, 1.0, 0.0).astype(BF16)


def _seg_sum(x, bd):
    hi, lo = _split2(x)
    return (jnp.dot(hi, bd, preferred_element_type=F32)
            + jnp.dot(lo, bd, preferred_element_type=F32))


def _ln_lanes(x, g, b):
    mu = jnp.mean(x, axis=-1, keepdims=True)
    d = x - mu
    var = jnp.mean(d * d, axis=-1, keepdims=True)
    return d * lax.rsqrt(var + LN_EPS) * g + b


def _mm_kernel(x_ref, w_ref, o_ref, *, act):
    acc = jnp.dot(x_ref[...], w_ref[...], preferred_element_type=F32)
    if act == "silu":
        acc = acc * jax.nn.sigmoid(acc)
    elif act == "sigmoid":
        acc = jax.nn.sigmoid(acc)
    o_ref[...] = acc.astype(o_ref.dtype)


def _pick_tile(n, candidates):
    for c in candidates:
        if n % c == 0:
            return c
    return n


def _matmul(x, w, act=None, name="proj"):
    m, k = x.shape
    n = w.shape[1]
    tm = _pick_tile(m, (1024,))
    tn = _pick_tile(n, (1024, 640, 512, 256, 128))
    return pl.pallas_call(
        functools.partial(_mm_kernel, act=act),
        grid=(m // tm, n // tn),
        in_specs=[pl.BlockSpec((tm, k), lambda i, j: (i, 0)),
                  pl.BlockSpec((k, tn), lambda i, j: (0, j))],
        out_specs=pl.BlockSpec((tm, tn), lambda i, j: (i, j)),
        out_shape=jax.ShapeDtypeStruct((m, n), F32),
        compiler_params=_params("parallel", "arbitrary"),
        name=name,
    )(x, w)


def _merge_kernel(ya_ref, yb_ref, yc_ref, yd_ref, g_ref, w_ref, o_ref, acc_ref):
    i = pl.program_id(1)

    @pl.when(i == 0)
    def _():
        acc_ref[...] = jnp.zeros_like(acc_ref)

    for b, y_ref in enumerate((ya_ref, yb_ref, yc_ref, yd_ref)):
        @pl.when(i == b)
        def _(y_ref=y_ref):
            acc_ref[...] += g_ref[...] * jnp.dot(y_ref[...], w_ref[0],
                                                 preferred_element_type=F32)

    @pl.when(i == N_BRANCH - 1)
    def _():
        o_ref[...] = acc_ref[...].astype(o_ref.dtype)


def _merge(ys, gate, w_branch):
    m = gate.shape[0]
    tm = _pick_tile(m, (512,))
    y_spec = pl.BlockSpec((tm, BRANCH_W), lambda i, j: (i, 0))
    return pl.pallas_call(
        _merge_kernel,
        grid=(m // tm, N_BRANCH),
        in_specs=[y_spec, y_spec, y_spec, y_spec,
                  pl.BlockSpec((tm, D_MODEL), lambda i, j: (i, j)),
                  pl.BlockSpec((1, BRANCH_W, D_MODEL), lambda i, j: (j, 0, 0))],
        out_specs=pl.BlockSpec((tm, D_MODEL), lambda i, j: (i, 0)),
        out_shape=jax.ShapeDtypeStruct((m, D_MODEL), BF16),
        scratch_shapes=[pltpu.VMEM((tm, D_MODEL), F32)],
        compiler_params=_params("parallel", "arbitrary"),
        name="merge",
    )(*ys, gate, w_branch)


def _outln_kernel(m_ref, w_ref, x_ref, g_ref, b_ref, of_ref, ob_ref):
    out = jnp.dot(m_ref[...], w_ref[...], preferred_element_type=F32)
    y = _ln_lanes(ALPHA * x_ref[...] + out, g_ref[...], b_ref[...])
    of_ref[...] = y
    ob_ref[...] = y.astype(BF16)


def _out_ln(merged, w_out, x, g, b):
    m = x.shape[0]
    tm = _pick_tile(m, (512,))
    row = pl.BlockSpec((tm, D_MODEL), lambda i: (i, 0))
    return pl.pallas_call(
        _outln_kernel,
        grid=(m // tm,),
        in_specs=[row, _full((D_MODEL, D_MODEL)), row, _full((1, D_MODEL)), _full((1, D_MODEL))],
        out_specs=[row, row],
        out_shape=[jax.ShapeDtypeStruct((m, D_MODEL), F32),
                   jax.ShapeDtypeStruct((m, D_MODEL), BF16)],
        compiler_params=_params("parallel"),
        name="out_ln",
    )(merged, w_out, x, g, b)


N_PAIR = A_HEADS // 2


def _rwkv_kernel(pa_ref, sh_ref, s0_ref, z_ref, mu_ref, w0_ref, a0_ref, wl_ref, kk_ref, ka_ref,
                 rk_ref, g_ref, b_ref, y_ref, sout_ref,
                 s_scr, prev_scr, r_s, w_s, k_s, v_s, kap_s, beta_s, y_s, *, lt):
    t = pl.program_id(1)
    bd = _pair_ones()

    @pl.when(t == 0)
    def _():
        prev_scr[...] = sh_ref[0, 0]
        for p in range(N_PAIR):
            s_scr[p] = jnp.concatenate([s0_ref[0, 0, 2 * p], s0_ref[0, 0, 2 * p + 1]], axis=1)

    pa = pa_ref[0]
    row = lax.broadcasted_iota(jnp.int32, pa.shape, 0)
    prev = jnp.where(row == 0, prev_scr[...], pltpu.roll(pa, 1, axis=0))
    prev_scr[...] = pa[lt - 1:lt, :]
    xs = pa + (prev - pa) * mu_ref[...]
    w3 = 3 * BRANCH_W
    lo = xs[:, w3:]
    lane = lax.broadcasted_iota(jnp.int32, lo.shape, 1)
    lo = jnp.where(lane < A_DECAY_LORA, jnp.tanh(lo), lo)
    pre = jnp.dot(lo.astype(BF16), wl_ref[...], preferred_element_type=F32)
    for p in range(N_PAIR):
        sl = slice(p * LANES, (p + 1) * LANES)
        r = xs[:, sl]
        k = xs[:, BRANCH_W + p * LANES:BRANCH_W + (p + 1) * LANES]
        v = xs[:, 2 * BRANCH_W + p * LANES:2 * BRANCH_W + (p + 1) * LANES]
        w_log = -_softplus(-(w0_ref[:, sl] + pre[:, sl])) - 0.5
        a = jax.nn.sigmoid(a0_ref[:, sl] + pre[:, BRANCH_W + p * LANES:BRANCH_W + (p + 1) * LANES])
        kk = k * kk_ref[:, sl]
        kk = kk / jnp.maximum(jnp.sqrt(_seg_sum(kk * kk, bd)), 1e-12)
        r_s[:, sl] = r
        w_s[:, sl] = jnp.exp(-jnp.exp(w_log))
        k_s[:, sl] = k * (1.0 + (a - 1.0) * ka_ref[:, sl])
        v_s[:, sl] = v
        kap_s[:, sl] = kk
        beta_s[:, sl] = kk * a

    vi = lax.broadcasted_iota(jnp.int32, (A_HD, LANES), 0)
    li = lax.broadcasted_iota(jnp.int32, (A_HD, LANES), 1)
    diag = (li % A_HD) == vi

    def steps(c, carry):
        r0 = pl.multiple_of(c * SUBLANES, SUBLANES)
        rows = pl.ds(r0, SUBLANES)
        for p in range(N_PAIR):
            sl = slice(p * LANES, (p + 1) * LANES)
            r8, w8, k8, v8 = r_s[rows, sl], w_s[rows, sl], k_s[rows, sl], v_s[rows, sl]
            kap8, beta8 = kap_s[rows, sl], beta_s[rows, sl]
            s = s_scr[p]
            ys = []
            for j in range(SUBLANES):
                vd = jnp.where(diag, v8[j:j + 1], 0.0)
                p_hi, p_lo = _split2(s * kap8[j:j + 1])
                v_hi, v_lo = _split2(vd)
                res = jnp.dot(jnp.concatenate([p_hi, p_lo, v_hi, v_lo], axis=0), bd,
                              preferred_element_type=F32)
                u = res[0:A_HD] + res[A_HD:2 * A_HD]
                vb = res[2 * A_HD:3 * A_HD] + res[3 * A_HD:]
                s = s * w8[j:j + 1] - u * beta8[j:j + 1] + vb * k8[j:j + 1]
                q_hi, q_lo = _split2(s * r8[j:j + 1])
                res = jnp.dot(jnp.concatenate([q_hi, q_lo], axis=0), bd,
                              preferred_element_type=F32)
                yb = res[0:A_HD] + res[A_HD:]
                ys.append(jnp.sum(jnp.where(diag, yb, 0.0), axis=0, keepdims=True))
            s_scr[p] = s
            y_s[rows, sl] = jnp.concatenate(ys, axis=0)
        return carry

    lax.fori_loop(0, lt // SUBLANES, steps, 0)

    inv_hd = 1.0 / A_HD
    for p in range(N_PAIR):
        sl = slice(p * LANES, (p + 1) * LANES)
        y = y_s[:, sl]
        d = y - _seg_sum(y, bd) * inv_hd
        var = _seg_sum(d * d, bd) * inv_hd
        yn = d * lax.rsqrt(var + LN_EPS) * g_ref[:, sl] + b_ref[:, sl]
        bonus = _seg_sum(r_s[:, sl] * k_s[:, sl] * rk_ref[:, sl], bd) * v_s[:, sl]
        y_ref[0, :, sl] = ((yn + bonus) * z_ref[0, :, sl]).astype(BF16)

    @pl.when(t == pl.num_programs(1) - 1)
    def _():
        for p in range(N_PAIR):
            s = s_scr[p]
            sout_ref[0, 2 * p] = s[:, :A_HD]
            sout_ref[0, 2 * p + 1] = s[:, A_HD:]


def _rwkv(pa, z, shift, s0, layer, wts):
    bn, ln, _ = pa.shape
    lt = _pick_tile(ln, (256,))
    row = lambda n: _full((1, n))
    tok = lambda n, j: pl.BlockSpec((1, lt, n), lambda b, t: (b, t, j))
    scr = lambda: pltpu.VMEM((lt, BRANCH_W), F32)
    return pl.pallas_call(
        functools.partial(_rwkv_kernel, lt=lt),
        grid=(bn, ln // lt),
        in_specs=[tok(A_COLS, 0),
                  pl.BlockSpec((1, 1, 1, A_COLS), lambda b, t: (layer, b, 0, 0)),
                  pl.BlockSpec((1, 1, A_HEADS, A_HD, A_HD), lambda b, t: (layer, b, 0, 0, 0)),
                  tok(BRANCH_W, 0),
                  row(A_COLS), row(BRANCH_W), row(BRANCH_W), _full((LANES, 2 * BRANCH_W)),
                  row(BRANCH_W), row(BRANCH_W), row(BRANCH_W), row(BRANCH_W), row(BRANCH_W)],
        out_specs=[tok(BRANCH_W, 0),
                   pl.BlockSpec((1, A_HEADS, A_HD, A_HD), lambda b, t: (b, 0, 0, 0))],
        out_shape=[jax.ShapeDtypeStruct((bn, ln, BRANCH_W), BF16),
                   jax.ShapeDtypeStruct((bn, A_HEADS, A_HD, A_HD), F32)],
        scratch_shapes=[pltpu.VMEM((N_PAIR, A_HD, LANES), F32), pltpu.VMEM((1, A_COLS), F32),
                        scr(), scr(), scr(), scr(), scr(), scr(), scr()],
        compiler_params=_params("parallel", "arbitrary"),
        name="rwkv7",
    )(pa, shift, s0, z, wts["a_mu"], wts["a_w0"], wts["a_a0"], wts["a_lora"], wts["a_k_k"],
      wts["a_k_a"], wts["a_r_k"], wts["a_ln_g"], wts["a_ln_b"])


def _rope_swap(x, half):
    w = x.shape[-1]
    if 2 * half == w:
        return pltpu.roll(x, half, axis=1)
    lane = lax.broadcasted_iota(jnp.int32, x.shape, 1)
    return jnp.where(lane % (2 * half) < half,
                     pltpu.roll(x, w - half, axis=1), pltpu.roll(x, half, axis=1))


def _swa_kernel(sink_ref, q_ref, k_ref, v_ref, z_ref, cs_ref, sn_ref, kb_ref, vb_ref,
                y_ref, ko_ref, pk_scr, pv_scr, *, bb_n, lb, nb, pos0):
    n = pl.program_id(1)

    @pl.when(n == 0)
    def _():
        pk_scr[...] = kb_ref[0]
        pv_scr[...] = vb_ref[0]

    cs = cs_ref[...]
    sn = sn_ref[...]
    cs_q = jnp.concatenate([cs] * (B_HEADS * B_HD // LANES), axis=1)
    sn_q = jnp.concatenate([sn] * (B_HEADS * B_HD // LANES), axis=1)
    a_idx = lax.broadcasted_iota(jnp.int32, (lb, WINDOW + lb), 0)
    c_idx = lax.broadcasted_iota(jnp.int32, (lb, WINDOW + lb), 1)
    key_pos = pos0 + n * lb - WINDOW + c_idx
    mask = (c_idx >= a_idx) & (c_idx <= WINDOW + a_idx) & (key_pos >= 0)
    group = B_HEADS // B_KV_HEADS

    def per_b(bb, carry):
        q = q_ref[bb]
        q = q * cs_q + _rope_swap(q, B_HD // 2) * sn_q
        k = k_ref[bb]
        k = k * cs + _rope_swap(k, B_HD // 2) * sn
        v = v_ref[bb]
        ko_ref[bb] = k
        keys = jnp.concatenate([pk_scr[bb], k], axis=0).astype(BF16)
        vals = jnp.concatenate([pv_scr[bb], v], axis=0).astype(BF16)
        if nb > 1:
            pk_scr[bb] = k
            pv_scr[bb] = v
        outs = []
        for h in range(B_HEADS):
            g = h // group
            qh = q[:, h * B_HD:(h + 1) * B_HD].astype(BF16)
            s = lax.dot_general(qh, keys[:, g * B_HD:(g + 1) * B_HD], NT_DIMS,
                                preferred_element_type=F32) * (B_HD ** -0.5)
            s = jnp.where(mask, s, NEG_INF)
            sink = sink_ref[h]
            m = jnp.maximum(jnp.max(s, axis=-1, keepdims=True), sink)
            p = jnp.exp(s - m)
            p = p / (jnp.sum(p, axis=-1, keepdims=True) + jnp.exp(sink - m))
            outs.append(jnp.dot(p.astype(BF16), vals[:, g * B_HD:(g + 1) * B_HD],
                                preferred_element_type=F32))
        for j in range(B_HEADS // 2):
            sl = slice(j * LANES, (j + 1) * LANES)
            o = jnp.concatenate([outs[2 * j], outs[2 * j + 1]], axis=1)
            y_ref[bb, :, sl] = (o * z_ref[bb, :, sl]).astype(BF16)
        return carry

    lax.fori_loop(0, bb_n, per_b, 0)


def _swa(pb, z, k_buf, v_buf, layer, sinks, cos_t, sin_t, pos0):
    bn, ln, _ = pb.shape
    lb = WINDOW if ln % WINDOW == 0 else ln
    nb = ln // lb
    bb_n = 1 if nb > 1 else _pick_tile(bn, (8,))
    qw = B_HEADS * B_HD
    tok = lambda n, j: pl.BlockSpec((bb_n, lb, n), lambda b, t: (b, t, j))
    buf = pl.BlockSpec((1, bb_n, WINDOW, LANES), lambda b, t: (layer, b, 0, 0))
    tab = pl.BlockSpec((lb, LANES), lambda b, t: (t, 0))
    return pl.pallas_call(
        functools.partial(_swa_kernel, bb_n=bb_n, lb=lb, nb=nb, pos0=pos0),
        grid=(bn // bb_n, nb),
        in_specs=[pl.BlockSpec(memory_space=pltpu.SMEM),
                  tok(qw, 0), tok(LANES, qw // LANES), tok(LANES, qw // LANES + 1),
                  tok(BRANCH_W, 1), tab, tab, buf, buf],
        out_specs=[tok(BRANCH_W, 0), tok(LANES, 0)],
        out_shape=[jax.ShapeDtypeStruct((bn, ln, BRANCH_W), BF16),
                   jax.ShapeDtypeStruct((bn, ln, LANES), F32)],
        scratch_shapes=[pltpu.VMEM((bb_n, WINDOW, LANES), F32),
                        pltpu.VMEM((bb_n, WINDOW, LANES), F32)],
        compiler_params=_params("parallel", "arbitrary"),
        name="swa",
    )(sinks, pb, pb, pb, z, cos_t, sin_t, k_buf, v_buf)


def _cols_to_rows(x, n_rows):
    r = lax.broadcasted_iota(jnp.int32, (n_rows, LANES), 0)
    c = lax.broadcasted_iota(jnp.int32, (n_rows, LANES), 1)
    sel = jnp.where(r == c, 1.0, 0.0).astype(BF16)
    hi = x.astype(BF16)
    rem = x - hi.astype(F32)
    mid = rem.astype(BF16)
    lo = (rem - mid.astype(F32)).astype(BF16)
    out = lax.dot_general(sel, hi, NT_DIMS, preferred_element_type=F32)
    out = out + lax.dot_general(sel, mid, NT_DIMS, preferred_element_type=F32)
    return out + lax.dot_general(sel, lo, NT_DIMS, preferred_element_type=F32)


def _mlstm_kernel(qk_ref, v_ref, gt_ref, z_ref, cb_ref, cw_ref, cbias_ref, gb_ref,
                  c0_ref, n0_ref, m0_ref, g_ref, b_ref,
                  y_ref, cout_ref, nout_ref, mout_ref,
                  c_scr, n_scr, m_scr, prev_scr, act_scr, gl_scr, *, bb_n, lt, lc):
    t = pl.program_id(1)

    @pl.when(t == 0)
    def _():
        c_scr[...] = c0_ref[0]
        n_scr[...] = n0_ref[0]
        m_scr[...] = m0_ref[0]
        prev_scr[...] = cb_ref[0]

    row8 = lax.broadcasted_iota(jnp.int32, (SUBLANES, C_CONV_COLS), 0)
    lane = lax.broadcasted_iota(jnp.int32, (lt, LANES), 1)
    qi = lax.broadcasted_iota(jnp.int32, (lc, lc), 0)
    ki = lax.broadcasted_iota(jnp.int32, (lc, lc), 1)
    tril = ki <= qi
    qkw = C_HEADS * C_QK

    def per_b(bb, carry):
        u = qk_ref[bb]
        prev8 = prev_scr[bb]
        conv = cbias_ref[...] + u * cw_ref[C_CONV - 1:C_CONV, :]
        for s in range(1, C_CONV):
            sh = pltpu.roll(u, s, axis=0)
            head = jnp.where(row8 < s, pltpu.roll(prev8, s, axis=0), sh[:SUBLANES])
            sh = head if lt == SUBLANES else jnp.concatenate([head, sh[SUBLANES:]], axis=0)
            conv = conv + sh * cw_ref[C_CONV - 1 - s:C_CONV - s, :]
        act_scr[...] = conv * jax.nn.sigmoid(conv)
        prev_scr[bb] = u[lt - SUBLANES:, :]
        g = gt_ref[bb] + gb_ref[...]
        gl_scr[...] = jnp.where(lane < C_HEADS, g, -_softplus(-g))

        def chunk(r0):
            glc = gl_scr[pl.ds(r0, lc), :]
            glr = _cols_to_rows(glc, 2 * C_HEADS)
            actc = act_scr[pl.ds(r0, lc), :]
            vc = v_ref[bb, pl.ds(r0, lc), :]
            zc = z_ref[bb, pl.ds(r0, lc), :]
            for h in range(C_HEADS):
                li_col = glc[:, h:h + 1]
                lf_col = glc[:, C_HEADS + h:C_HEADS + h + 1]
                li_row = glr[h:h + 1, :]
                lf_row = glr[C_HEADS + h:C_HEADS + h + 1, :]
                b_col = jnp.sum(jnp.where(tril, lf_row, 0.0), axis=1, keepdims=True)
                b_row = jnp.sum(jnp.where(qi <= ki, lf_col, 0.0), axis=0, keepdims=True)
                m_s = m_scr[bb, h][:, :1]
                dmat = jnp.where(tril, b_col - b_row + li_row, NEG_INF)
                inter = b_col + m_s
                mt = jnp.maximum(inter, jnp.max(dmat, axis=1, keepdims=True))
                wts = jnp.exp(dmat - mt)
                sc = jnp.exp(inter - mt)
                q = actc[:, h * C_QK:(h + 1) * C_QK]
                k = actc[:, qkw + h * C_QK:qkw + (h + 1) * C_QK] * (C_QK ** -0.5)
                vsl = slice(h * C_V, (h + 1) * C_V)
                v = vc[:, vsl].astype(BF16)
                c_st = c_scr[bb, h]
                n_st = n_scr[bb, h]
                qb = q.astype(BF16)
                a = lax.dot_general(qb, k.astype(BF16), NT_DIMS, preferred_element_type=F32) * wts
                num = (jnp.dot(a.astype(BF16), v, preferred_element_type=F32)
                       + sc * jnp.dot(qb, c_st.astype(BF16), preferred_element_type=F32))
                den = (jnp.sum(a, axis=1, keepdims=True)
                       + sc * jnp.sum(q * n_st, axis=1, keepdims=True))
                hh = num / jnp.maximum(jnp.abs(den), jnp.exp(-mt))
                b_last = b_row[:, lc - 1:lc]
                gl_row = b_last - b_row + li_row
                gl_col = b_last - b_col + li_col
                m_new = jnp.maximum(b_last + m_s, jnp.max(gl_row, axis=1, keepdims=True))
                kw = k * jnp.exp(gl_col - m_new)
                s_old = jnp.exp(b_last + m_s - m_new)
                c_scr[bb, h] = s_old * c_st + lax.dot_general(kw.astype(BF16), v, TN_DIMS,
                                                              preferred_element_type=F32)
                n_scr[bb, h] = s_old * n_st + jnp.sum(kw, axis=0, keepdims=True)
                m_scr[bb, h] = jnp.broadcast_to(m_new, (1, LANES))
                yn = _ln_lanes(hh, g_ref[:, vsl], b_ref[:, vsl])
                y_ref[bb, pl.ds(r0, lc), vsl] = (yn * zc[:, vsl]).astype(BF16)

        if lt == lc:
            chunk(0)
        else:
            def body(c, cc):
                chunk(pl.multiple_of(c * lc, lc))
                return cc
            lax.fori_loop(0, lt // lc, body, 0)
        return carry

    lax.fori_loop(0, bb_n, per_b, 0)

    @pl.when(t == pl.num_programs(1) - 1)
    def _():
        cout_ref[...] = c_scr[...]
        nout_ref[...] = n_scr[...]
        mout_ref[...] = m_scr[...]


def _mlstm(qk, v, gates, z, conv8, c0, n0, m0, layer, wts):
    bn, ln, _ = qk.shape
    lc = CHUNK if ln % CHUNK == 0 else ln
    lt = _pick_tile(ln, (256,))
    bb_n = 1 if ln > lt or bn < SUBLANES else SUBLANES
    tok = lambda n, j: pl.BlockSpec((bb_n, lt, n), lambda b, t: (b, t, j))
    st_in = lambda *s: pl.BlockSpec((1, bb_n) + s, lambda b, t: (layer, b) + (0,) * len(s))
    st_out = lambda *s: pl.BlockSpec((bb_n,) + s, lambda b, t: (b,) + (0,) * len(s))
    row = lambda n: _full((1, n))
    return pl.pallas_call(
        functools.partial(_mlstm_kernel, bb_n=bb_n, lt=lt, lc=lc),
        grid=(bn // bb_n, ln // lt),
        in_specs=[tok(C_CONV_COLS, 0), tok(BRANCH_W, 0), tok(LANES, 0), tok(BRANCH_W, 2),
                  st_in(SUBLANES, C_CONV_COLS), _full((C_CONV, C_CONV_COLS)), row(C_CONV_COLS),
                  row(LANES),
                  st_in(C_HEADS, C_QK, C_V), st_in(C_HEADS, 1, C_QK), st_in(C_HEADS, 1, LANES),
                  row(BRANCH_W), row(BRANCH_W)],
        out_specs=[tok(BRANCH_W, 0), st_out(C_HEADS, C_QK, C_V), st_out(C_HEADS, 1, C_QK),
                   st_out(C_HEADS, 1, LANES)],
        out_shape=[jax.ShapeDtypeStruct((bn, ln, BRANCH_W), BF16),
                   jax.ShapeDtypeStruct((bn, C_HEADS, C_QK, C_V), F32),
                   jax.ShapeDtypeStruct((bn, C_HEADS, 1, C_QK), F32),
                   jax.ShapeDtypeStruct((bn, C_HEADS, 1, LANES), F32)],
        scratch_shapes=[pltpu.VMEM((bb_n, C_HEADS, C_QK, C_V), F32),
                        pltpu.VMEM((bb_n, C_HEADS, 1, C_QK), F32),
                        pltpu.VMEM((bb_n, C_HEADS, 1, LANES), F32),
                        pltpu.VMEM((bb_n, SUBLANES, C_CONV_COLS), F32),
                        pltpu.VMEM((lt, C_CONV_COLS), F32),
                        pltpu.VMEM((lt, LANES), F32)],
        compiler_params=_params("parallel", "arbitrary"),
        name="mlstm",
    )(qk, v, gates, z, conv8, wts["c_conv_w"], wts["c_conv_b"], wts["c_gate_b"],
      c0, n0, m0, wts["c_ln_g"], wts["c_ln_b"])


def _ret_kernel(q_ref, k_ref, v_ref, z_ref, cs_ref, sn_ref, dm_ref, qd_ref, kd_ref, cd_ref,
                s0_ref, g_ref, b_ref, y_ref, sout_ref, s_scr, *, bb_n, lt, lc):
    t = pl.program_id(2)

    @pl.when(t == 0)
    def _():
        s_scr[...] = s0_ref[0, :, 0]

    dm = dm_ref[0]
    qd = qd_ref[0]
    kd = kd_ref[0]
    cd = cd_ref[0]

    def per_b(bb, carry):
        def chunk(r0):
            cs = cs_ref[pl.ds(r0, lc), :]
            sn = sn_ref[pl.ds(r0, lc), :]
            q = q_ref[bb, pl.ds(r0, lc), :]
            k = k_ref[bb, pl.ds(r0, lc), :]
            q = (q * cs + _rope_swap(q, D_QK // 2) * sn).astype(BF16)
            k = (k * cs + _rope_swap(k, D_QK // 2) * sn) * (D_QK ** -0.5)
            v = v_ref[bb, pl.ds(r0, lc), :].astype(BF16)
            s = s_scr[bb]
            inner = lax.dot_general(q, k.astype(BF16), NT_DIMS, preferred_element_type=F32) * dm
            o = (jnp.dot(inner.astype(BF16), v, preferred_element_type=F32)
                 + jnp.dot(q, s.astype(BF16), preferred_element_type=F32) * qd)
            s_scr[bb] = cd * s + lax.dot_general((k * kd).astype(BF16), v, TN_DIMS,
                                                 preferred_element_type=F32)
            yn = _ln_lanes(o, g_ref[...], b_ref[...])
            y_ref[bb, pl.ds(r0, lc), :] = (yn * z_ref[bb, pl.ds(r0, lc), :]).astype(BF16)

        if lt == lc:
            chunk(0)
        else:
            def body(c, cc):
                chunk(pl.multiple_of(c * lc, lc))
                return cc
            lax.fori_loop(0, lt // lc, body, 0)
        return carry

    lax.fori_loop(0, bb_n, per_b, 0)

    @pl.when(t == pl.num_programs(2) - 1)
    def _():
        sout_ref[:, 0] = s_scr[...]


def _retention(pd, z, s0, layer, wts, cos_t, sin_t):
    bn, ln, _ = pd.shape
    lc = CHUNK if ln % CHUNK == 0 else ln
    lt = _pick_tile(ln, (512,))
    bb_n = 1 if ln > lt or bn < SUBLANES else SUBLANES
    tok = lambda off: pl.BlockSpec((bb_n, lt, LANES), lambda b, h, t: (b, t, off + h))
    tab = pl.BlockSpec((lt, LANES), lambda b, h, t: (t, 0))
    per_h = lambda r: pl.BlockSpec((1, r, LANES), lambda b, h, t: (h, 0, 0))
    dec = wts["d_tables"][lc]
    return pl.pallas_call(
        functools.partial(_ret_kernel, bb_n=bb_n, lt=lt, lc=lc),
        grid=(bn // bb_n, D_HEADS, ln // lt),
        in_specs=[tok(0), tok(D_HEADS), tok(2 * D_HEADS), tok(3 * D_HEADS),
                  tab, tab,
                  pl.BlockSpec((1, lc, lc), lambda b, h, t: (h, 0, 0)),
                  per_h(lc), per_h(lc), per_h(1),
                  pl.BlockSpec((1, bb_n, 1, D_QK, D_V), lambda b, h, t: (layer, b, h, 0, 0)),
                  pl.BlockSpec((1, LANES), lambda b, h, t: (0, h)),
                  pl.BlockSpec((1, LANES), lambda b, h, t: (0, h))],
        out_specs=[pl.BlockSpec((bb_n, lt, LANES), lambda b, h, t: (b, t, h)),
                   pl.BlockSpec((bb_n, 1, D_QK, D_V), lambda b, h, t: (b, h, 0, 0))],
        out_shape=[jax.ShapeDtypeStruct((bn, ln, BRANCH_W), BF16),
                   jax.ShapeDtypeStruct((bn, D_HEADS, D_QK, D_V), F32)],
        scratch_shapes=[pltpu.VMEM((bb_n, D_QK, D_V), F32)],
        compiler_params=_params("parallel", "parallel", "arbitrary"),
        name="retention",
    )(pd, pd, pd, z, cos_t, sin_t, dec["decay_mat"], dec["q_dec"], dec["k_dec"], dec["c_dec"],
      s0, wts["d_ln_g"], wts["d_ln_b"])


def _rope_tables(pos, d):
    inv = ROPE_THETA ** (-jnp.arange(0, d, 2, dtype=F32) / d)
    ang = pos.astype(F32)[:, None] * inv[None, :]
    cos = jnp.cos(ang)
    sin = jnp.sin(ang)
    reps = LANES // d
    return (jnp.tile(jnp.concatenate([cos, cos], -1), (1, reps)),
            jnp.tile(jnp.concatenate([-sin, sin], -1), (1, reps)))


def _retention_tables(lc):
    log_gamma = jnp.log1p(-jnp.exp2(-5.0 - jnp.arange(D_HEADS, dtype=F32)))
    idx = jnp.arange(lc, dtype=F32)
    rel = idx[:, None] - idx[None, :]
    decay_mat = jnp.where(rel >= 0, jnp.exp(jnp.maximum(rel, 0.0) * log_gamma[:, None, None]), 0.0)
    q_dec = jnp.exp((idx + 1.0)[None, :] * log_gamma[:, None])
    k_dec = jnp.exp((lc - 1.0 - idx)[None, :] * log_gamma[:, None])
    c_dec = jnp.exp(lc * log_gamma)
    bcast = lambda a: jnp.broadcast_to(a[..., None], a.shape + (LANES,))
    return {"decay_mat": decay_mat, "q_dec": bcast(q_dec), "k_dec": bcast(k_dec),
            "c_dec": bcast(c_dec[:, None])}


def _layer_weights(l, w_in, a_mu, a_w0, a_w_up, a_a0, a_a_up, a_k_k, a_k_a, a_r_k, a_ln_g, a_ln_b,
                   b_sinks, c_conv_w, c_conv_b, c_i_bias, c_f_bias, c_ln_g, c_ln_b,
                   d_ln_g, d_ln_b, w_branch, w_out, ln_g, ln_b, chunk_lens):
    o1 = A_COLS
    o2 = o1 + B_COLS
    o3 = o2 + C_COLS
    o4 = MIX_COLS
    o5 = o4 + N_BRANCH * BRANCH_W
    og = o2 + C_CONV_COLS + C_HEADS * C_V
    wl = w_in[l]
    seg = lambda a, b: wl[:, a:b].astype(BF16)
    row = lambda a: a[l].reshape(1, -1)
    zeros = jnp.zeros((A_DECAY_LORA, BRANCH_W), F32)
    lora = jnp.concatenate([jnp.concatenate([a_w_up[l], zeros], 1),
                            jnp.concatenate([zeros, a_a_up[l]], 1)], 0).astype(BF16)
    gate_b = jnp.pad(jnp.concatenate([c_i_bias[l], c_f_bias[l]]), (0, LANES - 2 * C_HEADS))
    return {
        "w_a": seg(0, o1), "w_b": seg(o1, o2), "w_cqk": seg(o2, o2 + C_CONV_COLS),
        "w_cv": seg(o2 + C_CONV_COLS, og),
        "w_cg": jnp.pad(wl[:, og:o3], ((0, 0), (0, LANES - 2 * C_HEADS))).astype(BF16),
        "w_d": seg(o3, o4), "w_z": seg(o4, o5), "w_g": seg(o5, wl.shape[1]),
        "a_mu": row(a_mu), "a_w0": row(a_w0), "a_a0": row(a_a0), "a_lora": lora,
        "a_k_k": row(a_k_k), "a_k_a": row(a_k_a), "a_r_k": row(a_r_k),
        "a_ln_g": row(a_ln_g), "a_ln_b": row(a_ln_b),
        "b_sinks": b_sinks[l],
        "c_conv_w": c_conv_w[l], "c_conv_b": row(c_conv_b), "c_gate_b": gate_b.reshape(1, LANES),
        "c_ln_g": row(c_ln_g), "c_ln_b": row(c_ln_b),
        "d_ln_g": row(d_ln_g), "d_ln_b": row(d_ln_b),
        "d_tables": {lc: _retention_tables(lc) for lc in chunk_lens},
        "w_branch": w_branch[l].astype(BF16), "w_out": w_out[l].astype(BF16),
        "ln_g": row(ln_g), "ln_b": row(ln_b),
    }


def _hybrid_layer(xf, xb, pos0, l, st, wts, tabs):
    bn, ln, _ = xf.shape
    m = bn * ln
    s_a, shift_a, k_buf, v_buf, c_m, n_m, m_m, conv_m, s_d = st
    tok = lambda a: a.reshape(bn, ln, a.shape[-1])
    pa = tok(_matmul(xb, wts["w_a"], name="proj_a"))
    pb = tok(_matmul(xb, wts["w_b"], name="proj_b"))
    pcqk = tok(_matmul(xb, wts["w_cqk"], name="proj_cqk"))
    pcv = tok(_matmul(xb, wts["w_cv"], name="proj_cv"))
    pcg = tok(_matmul(xb, wts["w_cg"], name="proj_cg"))
    pd = tok(_matmul(xb, wts["w_d"], name="proj_d"))
    z = tok(_matmul(xb, wts["w_z"], act="silu", name="proj_z"))
    gate = _matmul(xb, wts["w_g"], act="sigmoid", name="proj_gate")

    y_a, s_a_new = _rwkv(pa, z, shift_a, s_a, l, wts)
    shift_new = pa[:, -1]

    y_b, k_rot = _swa(pb, z, k_buf, v_buf, l, wts["b_sinks"], tabs["cos64"], tabs["sin64"], pos0)
    qw = B_HEADS * B_HD
    kvw = B_KV_HEADS * B_HD
    kv_shape = (bn, WINDOW, B_KV_HEADS, B_HD)
    k_new = jnp.concatenate([k_buf[l], k_rot], 1)[:, -WINDOW:].reshape(kv_shape)
    v_new = jnp.concatenate([v_buf[l], pb[..., qw + kvw:]], 1)[:, -WINDOW:].reshape(kv_shape)

    y_c, c_new, n_new, m_new = _mlstm(pcqk, pcv, pcg, z, conv_m, c_m, n_m, m_m, l, wts)
    conv_new = jnp.concatenate([conv_m[l, :, SUBLANES - (C_CONV - 1):], pcqk], 1)[:, -(C_CONV - 1):]

    y_d, s_d_new = _retention(pd, z, s_d, l, wts, tabs["cos128"], tabs["sin128"])

    flat = lambda a: a.reshape(m, a.shape[-1])
    merged = _merge([flat(y_a), flat(y_b), flat(y_c), flat(y_d)], gate, wts["w_branch"])
    xf_new, xb_new = _out_ln(merged, wts["w_out"], flat(xf), wts["ln_g"], wts["ln_b"])
    new = (s_a_new, shift_new, k_new, v_new, c_new, n_new[:, :, 0], m_new[:, :, 0, 0],
           conv_new, s_d_new)
    return xf_new.reshape(bn, ln, D_MODEL), xb_new, new


def _trunk(x, pos0, states, layer_wts):
    bn, ln, _ = x.shape
    s_a, shift_a, k_buf, v_buf, c_m, n_m, m_m, conv_m, s_d = states
    depth = s_a.shape[0]
    st = (s_a,
          shift_a[:, :, None, :],
          k_buf.reshape(depth, bn, WINDOW, LANES),
          v_buf.reshape(depth, bn, WINDOW, LANES),
          c_m,
          n_m[:, :, :, None, :],
          jnp.broadcast_to(m_m[..., None, None], m_m.shape + (1, LANES)),
          jnp.pad(conv_m, ((0, 0), (0, 0), (SUBLANES - (C_CONV - 1), 0), (0, 0))),
          s_d)
    pos = pos0 + jnp.arange(ln)
    cos64, sin64 = _rope_tables(pos, B_HD)
    cos128, sin128 = _rope_tables(pos, D_QK)
    tabs = {"cos64": cos64, "sin64": sin64, "cos128": cos128, "sin128": sin128}
    xb = x.reshape(bn * ln, D_MODEL).astype(BF16)
    new = []
    for l in range(depth):
        x, xb, st_new = _hybrid_layer(x, xb, pos0, l, st, layer_wts[l], tabs)
        new.append(st_new)
    stacked = tuple(jnp.stack([s[j] for s in new]) for j in range(len(states)))
    return x, stacked


def kernel(x_prompt, x_sample, state_rwkv_S, state_rwkv_shift, cache_swa_k, cache_swa_v,
           state_mlstm_C, state_mlstm_n, state_mlstm_m, state_mlstm_conv, state_ret_S,
           w_in, a_mu, a_w0, a_w_up, a_a0, a_a_up, a_k_k, a_k_a, a_r_k, a_ln_g, a_ln_b,
           b_sinks, c_conv_w, c_conv_b, c_i_bias, c_f_bias, c_ln_g, c_ln_b,
           d_ln_g, d_ln_b, w_branch, w_out, ln_g, ln_b):
    weights = (w_in, a_mu, a_w0, a_w_up, a_a0, a_a_up, a_k_k, a_k_a, a_r_k, a_ln_g, a_ln_b,
               b_sinks, c_conv_w, c_conv_b, c_i_bias, c_f_bias, c_ln_g, c_ln_b,
               d_ln_g, d_ln_b, w_branch, w_out, ln_g, ln_b)
    sample_states = (state_rwkv_S, state_rwkv_shift, cache_swa_k, cache_swa_v,
                     state_mlstm_C, state_mlstm_n, state_mlstm_m, state_mlstm_conv, state_ret_S)
    depth = w_in.shape[0]
    chunk_len = lambda ln: CHUNK if ln % CHUNK == 0 else ln
    chunk_lens = {chunk_len(x_prompt.shape[1]), chunk_len(x_sample.shape[1])}
    layer_wts = [_layer_weights(l, *weights, chunk_lens) for l in range(depth)]
    n_prompt = x_prompt.shape[0]
    zero_states = tuple(jnp.zeros((depth, n_prompt) + s.shape[2:], s.dtype) for s in sample_states)
    y_prompt, p = _trunk(x_prompt, 0, zero_states, layer_wts)
    y_sample, s = _trunk(x_sample, PAST_LEN, sample_states, layer_wts)
    return (y_prompt, y_sample, p[0], s[0], p[1], s[1], p[2], s[2], p[3], s[3], p[4], s[4],
            p[5], s[5], p[6], s[6], p[7], s[7], p[8], s[8])
```

```python
import functools

import jax
import jax.numpy as jnp
from jax import lax
from jax.experimental import pallas as pl
from jax.experimental.pallas import tpu as pltpu

F32 = jnp.float32
BF16 = jnp.bfloat16

D_MODEL = 2048
DEPTH = 2
PAST_LEN = 8192
N_BRANCH = 4
BRANCH_W = D_MODEL // 2
A_HD = 64
A_HEADS = BRANCH_W // A_HD
A_DECAY_LORA = 64
A_AAA_LORA = 64
B_HD = 64
B_HEADS = BRANCH_W // B_HD
B_KV_HEADS = B_HEADS // 8
WINDOW = 128
ROPE_THETA = 10000.0
C_HEADS = 8
C_V = BRANCH_W // C_HEADS
C_QK = C_V // 2
C_CONV = 4
D_HEADS = 8
D_QK = BRANCH_W // D_HEADS
D_V = BRANCH_W // D_HEADS
CHUNK = 64
LN_EPS = 1e-5
NEG_INF = -1e30
ALPHA = (2 * DEPTH) ** 0.25

A_COLS = 3 * BRANCH_W + A_DECAY_LORA + A_AAA_LORA
B_COLS = (B_HEADS + 2 * B_KV_HEADS) * B_HD
C_CONV_COLS = 2 * C_HEADS * C_QK
C_COLS = C_CONV_COLS + C_HEADS * C_V + 2 * C_HEADS
D_COLS = D_HEADS * (2 * D_QK + D_V)
MIX_COLS = A_COLS + B_COLS + C_COLS + D_COLS

LANES = 128
SUBLANES = 8
VMEM_LIMIT_BYTES = 56 * 1024 * 1024

NT_DIMS = (((1,), (1,)), ((), ()))
TN_DIMS = (((0,), (0,)), ((), ()))


def _params(*sem):
    return pltpu.CompilerParams(dimension_semantics=sem, vmem_limit_bytes=VMEM_LIMIT_BYTES)


def _full(shape):
    n = len(shape)
    return pl.BlockSpec(shape, lambda *_: (0,) * n)


def _softplus(x):
    return jnp.maximum(x, 0.0) + jnp.log1p(jnp.exp(-jnp.abs(x)))


def _split2(x):
    hi = x.astype(BF16)
    lo = (x - hi.astype(F32)).astype(BF16)
    return hi, lo


def _pair_ones():
    r = lax.broadcasted_iota(jnp.int32, (LANES, LANES), 0) // A_HD
    c = lax.broadcasted_iota(jnp.int32, (LANES, LANES), 1) // A_HD
    return jnp.where(r == c, 1.0, 0.0).astype(BF16)


def _seg_sum(x, bd):
    hi, lo = _split2(x)
    return (jnp.dot(hi, bd, preferred_element_type=F32)
            + jnp.dot(lo, bd, preferred_element_type=F32))


def _ln_lanes(x, g, b):
    mu = jnp.mean(x, axis=-1, keepdims=True)
    d = x - mu
    var = jnp.mean(d * d, axis=-1, keepdims=True)
    return d * lax.rsqrt(var + LN_EPS) * g + b


def _mm_kernel(x_ref, w_ref, o_ref, *, act):
    acc = jnp.dot(x_ref[...], w_ref[...], preferred_element_type=F32)
    if act == "silu":
        acc = acc * jax.nn.sigmoid(acc)
    elif act == "sigmoid":
        acc = jax.nn.sigmoid(acc)
    o_ref[...] = acc.astype(o_ref.dtype)


def _pick_tile(n, candidates):
    for c in candidates:
        if n % c == 0:
            return c
    return n


def _matmul(x, w, act=None, name="proj"):
    m, k = x.shape
    n = w.shape[1]
    tm = _pick_tile(m, (1024,))
    tn = _pick_tile(n, (1024, 640, 512, 256, 128))
    return pl.pallas_call(
        functools.partial(_mm_kernel, act=act),
        grid=(m // tm, n // tn),
        in_specs=[pl.BlockSpec((tm, k), lambda i, j: (i, 0)),
                  pl.BlockSpec((k, tn), lambda i, j: (0, j))],
        out_specs=pl.BlockSpec((tm, tn), lambda i, j: (i, j)),
        out_shape=jax.ShapeDtypeStruct((m, n), F32),
        compiler_params=_params("parallel", "arbitrary"),
        name=name,
    )(x, w)


def _merge_kernel(ya_ref, yb_ref, yc_ref, yd_ref, g_ref, w_ref, o_ref, acc_ref):
    i = pl.program_id(1)

    @pl.when(i == 0)
    def _():
        acc_ref[...] = jnp.zeros_like(acc_ref)

    for b, y_ref in enumerate((ya_ref, yb_ref, yc_ref, yd_ref)):
        @pl.when(i == b)
        def _(y_ref=y_ref):
            acc_ref[...] += g_ref[...] * jnp.dot(y_ref[...], w_ref[0],
                                                 preferred_element_type=F32)

    @pl.when(i == N_BRANCH - 1)
    def _():
        o_ref[...] = acc_ref[...].astype(o_ref.dtype)


def _merge(ys, gate, w_branch):
    m = gate.shape[0]
    tm = _pick_tile(m, (512,))
    y_spec = pl.BlockSpec((tm, BRANCH_W), lambda i, j: (i, 0))
    return pl.pallas_call(
        _merge_kernel,
        grid=(m // tm, N_BRANCH),
        in_specs=[y_spec, y_spec, y_spec, y_spec,
                  pl.BlockSpec((tm, D_MODEL), lambda i, j: (i, j)),
                  pl.BlockSpec((1, BRANCH_W, D_MODEL), lambda i, j: (j, 0, 0))],
        out_specs=pl.BlockSpec((tm, D_MODEL), lambda i, j: (i, 0)),
        out_shape=jax.ShapeDtypeStruct((m, D_MODEL), BF16),
        scratch_shapes=[pltpu.VMEM((tm, D_MODEL), F32)],
        compiler_params=_params("parallel", "arbitrary"),
        name="merge",
    )(*ys, gate, w_branch)


def _outln_kernel(m_ref, w_ref, x_ref, g_ref, b_ref, of_ref, ob_ref):
    out = jnp.dot(m_ref[...], w_ref[...], preferred_element_type=F32)
    y = _ln_lanes(ALPHA * x_ref[...] + out, g_ref[...], b_ref[...])
    of_ref[...] = y
    ob_ref[...] = y.astype(BF16)


def _out_ln(merged, w_out, x, g, b):
    m = x.shape[0]
    tm = _pick_tile(m, (512,))
    row = pl.BlockSpec((tm, D_MODEL), lambda i: (i, 0))
    return pl.pallas_call(
        _outln_kernel,
        grid=(m // tm,),
        in_specs=[row, _full((D_MODEL, D_MODEL)), row, _full((1, D_MODEL)), _full((1, D_MODEL))],
        out_specs=[row, row],
        out_shape=[jax.ShapeDtypeStruct((m, D_MODEL), F32),
                   jax.ShapeDtypeStruct((m, D_MODEL), BF16)],
        compiler_params=_params("parallel"),
        name="out_ln",
    )(merged, w_out, x, g, b)


N_PAIR = A_HEADS // 2
N_QUAD = N_PAIR // 2


def _quad_ones():
    n = 2 * LANES
    r = lax.broadcasted_iota(jnp.int32, (n, n), 0) // A_HD
    c = lax.broadcasted_iota(jnp.int32, (n, n), 1) // A_HD
    return jnp.where(r == c, 1.0, 0.0).astype(BF16)


def _rwkv_kernel(pa_ref, sh_ref, s0_ref, z_ref, mu_ref, w0_ref, a0_ref, wl_ref, kk_ref, ka_ref,
                 rk_ref, g_ref, b_ref, y_ref, sout_ref,
                 s_scr, prev_scr, r_s, w_s, k_s, v_s, kap_s, beta_s, y_s, *, bb_n, lt):
    t = pl.program_id(1)
    bd = _pair_ones()
    bd2 = _quad_ones()

    @pl.when(t == 0)
    def _():
        prev_scr[...] = sh_ref[0]
        for bb in range(bb_n):
            for p in range(N_PAIR):
                s_scr[bb, p] = jnp.concatenate([s0_ref[0, bb, 2 * p], s0_ref[0, bb, 2 * p + 1]],
                                               axis=1)

    w3 = 3 * BRANCH_W
    for bb in range(bb_n):
        pa = pa_ref[bb]
        row = lax.broadcasted_iota(jnp.int32, pa.shape, 0)
        prev = jnp.where(row == 0, prev_scr[bb], pltpu.roll(pa, 1, axis=0))
        prev_scr[bb] = pa[lt - 1:lt, :]
        xs = pa + (prev - pa) * mu_ref[...]
        lo = xs[:, w3:]
        lane = lax.broadcasted_iota(jnp.int32, lo.shape, 1)
        lo = jnp.where(lane < A_DECAY_LORA, jnp.tanh(lo), lo)
        pre = jnp.dot(lo.astype(BF16), wl_ref[...], preferred_element_type=F32)
        for p in range(N_PAIR):
            sl = slice(p * LANES, (p + 1) * LANES)
            r = xs[:, sl]
            k = xs[:, BRANCH_W + p * LANES:BRANCH_W + (p + 1) * LANES]
            v = xs[:, 2 * BRANCH_W + p * LANES:2 * BRANCH_W + (p + 1) * LANES]
            w_log = -_softplus(-(w0_ref[:, sl] + pre[:, sl])) - 0.5
            a = jax.nn.sigmoid(a0_ref[:, sl]
                               + pre[:, BRANCH_W + p * LANES:BRANCH_W + (p + 1) * LANES])
            kk = k * kk_ref[:, sl]
            kk = kk / jnp.maximum(jnp.sqrt(_seg_sum(kk * kk, bd)), 1e-12)
            r_s[bb, :, sl] = r
            w_s[bb, :, sl] = jnp.exp(-jnp.exp(w_log))
            k_s[bb, :, sl] = k * (1.0 + (a - 1.0) * ka_ref[:, sl])
            v_s[bb, :, sl] = v
            kap_s[bb, :, sl] = kk
            beta_s[bb, :, sl] = kk * a

    vi = lax.broadcasted_iota(jnp.int32, (A_HD, LANES), 0)
    li = lax.broadcasted_iota(jnp.int32, (A_HD, LANES), 1)
    diag = (li % A_HD) == vi
    row8 = lax.broadcasted_iota(jnp.int32, (SUBLANES, LANES), 0)
    chains = [(bb, q) for bb in range(bb_n) for q in range(N_QUAD)]
    lanes_of = lambda p: slice(p * LANES, (p + 1) * LANES)

    def steps(c, carry):
        rows = pl.ds(pl.multiple_of(c * SUBLANES, SUBLANES), SUBLANES)
        ytile = {(bb, p): jnp.zeros((SUBLANES, LANES), F32)
                 for bb in range(bb_n) for p in range(N_PAIR)}
        for j in range(SUBLANES):
            rowj = lambda ref, bb, p: ref[bb, rows, lanes_of(p)][j:j + 1]
            res1 = {}
            for bb, q in chains:
                lhs, vds = [], []
                for p in (2 * q, 2 * q + 1):
                    p_hi, p_lo = _split2(s_scr[bb, p] * rowj(kap_s, bb, p))
                    lhs.append(jnp.concatenate([p_hi, p_lo], axis=1))
                    vds.append(jnp.where(diag, rowj(v_s, bb, p), 0.0).astype(BF16))
                lhs.append(jnp.concatenate(vds, axis=1))
                res1[bb, q] = jnp.dot(jnp.concatenate(lhs, axis=0), bd2,
                                      preferred_element_type=F32)
            res2 = {}
            for bb, q in chains:
                res = res1[bb, q]
                qs = []
                for i, p in enumerate((2 * q, 2 * q + 1)):
                    blk = res[i * A_HD:(i + 1) * A_HD]
                    u = blk[:, :LANES] + blk[:, LANES:]
                    vb = res[2 * A_HD:, lanes_of(i)]
                    s = (s_scr[bb, p] * rowj(w_s, bb, p) - u * rowj(beta_s, bb, p)
                         + vb * rowj(k_s, bb, p))
                    s_scr[bb, p] = s
                    qs.append((s * rowj(r_s, bb, p)).astype(BF16))
                res2[bb, q] = jnp.dot(jnp.concatenate(qs, axis=1), bd2,
                                      preferred_element_type=F32)
            for bb, q in chains:
                for i, p in enumerate((2 * q, 2 * q + 1)):
                    yb = res2[bb, q][:, lanes_of(i)]
                    y_row = jnp.sum(jnp.where(diag, yb, 0.0), axis=0, keepdims=True)
                    ytile[bb, p] = jnp.where(row8 == j, y_row, ytile[bb, p])
        for (bb, p), tile in ytile.items():
            y_s[bb, rows, lanes_of(p)] = tile
        return carry

    lax.fori_loop(0, lt // SUBLANES, steps, 0)

    inv_hd = 1.0 / A_HD
    for bb in range(bb_n):
        for p in range(N_PAIR):
            sl = lanes_of(p)
            y = y_s[bb, :, sl]
            d = y - _seg_sum(y, bd) * inv_hd
            var = _seg_sum(d * d, bd) * inv_hd
            yn = d * lax.rsqrt(var + LN_EPS) * g_ref[:, sl] + b_ref[:, sl]
            bonus = _seg_sum(r_s[bb, :, sl] * k_s[bb, :, sl] * rk_ref[:, sl], bd) * v_s[bb, :, sl]
            y_ref[bb, :, sl] = ((yn + bonus) * z_ref[bb, :, sl]).astype(BF16)

    @pl.when(t == pl.num_programs(1) - 1)
    def _():
        for bb in range(bb_n):
            for p in range(N_PAIR):
                s = s_scr[bb, p]
                sout_ref[bb, 2 * p] = s[:, :A_HD]
                sout_ref[bb, 2 * p + 1] = s[:, A_HD:]


def _rwkv(pa, z, shift, s0, layer, wts):
    bn, ln, _ = pa.shape
    lt = _pick_tile(ln, (256,))
    bb_n = _pick_tile(bn, (2,) if ln > lt else (4, 2))
    row = lambda n: _full((1, n))
    tok = lambda n, j: pl.BlockSpec((bb_n, lt, n), lambda b, t: (b, t, j))
    scr = lambda: pltpu.VMEM((bb_n, lt, BRANCH_W), F32)
    return pl.pallas_call(
        functools.partial(_rwkv_kernel, bb_n=bb_n, lt=lt),
        grid=(bn // bb_n, ln // lt),
        in_specs=[tok(A_COLS, 0),
                  pl.BlockSpec((1, bb_n, 1, A_COLS), lambda b, t: (layer, b, 0, 0)),
                  pl.BlockSpec((1, bb_n, A_HEADS, A_HD, A_HD), lambda b, t: (layer, b, 0, 0, 0)),
                  tok(BRANCH_W, 0),
                  row(A_COLS), row(BRANCH_W), row(BRANCH_W), _full((LANES, 2 * BRANCH_W)),
                  row(BRANCH_W), row(BRANCH_W), row(BRANCH_W), row(BRANCH_W), row(BRANCH_W)],
        out_specs=[tok(BRANCH_W, 0),
                   pl.BlockSpec((bb_n, A_HEADS, A_HD, A_HD), lambda b, t: (b, 0, 0, 0))],
        out_shape=[jax.ShapeDtypeStruct((bn, ln, BRANCH_W), BF16),
                   jax.ShapeDtypeStruct((bn, A_HEADS, A_HD, A_HD), F32)],
        scratch_shapes=[pltpu.VMEM((bb_n, N_PAIR, A_HD, LANES), F32),
                        pltpu.VMEM((bb_n, 1, A_COLS), F32),
                        scr(), scr(), scr(), scr(), scr(), scr(), scr()],
        compiler_params=_params("parallel", "arbitrary"),
        name="rwkv7",
    )(pa, shift, s0, z, wts["a_mu"], wts["a_w0"], wts["a_a0"], wts["a_lora"], wts["a_k_k"],
      wts["a_k_a"], wts["a_r_k"], wts["a_ln_g"], wts["a_ln_b"])


def _rope_swap(x, half):
    w = x.shape[-1]
    if 2 * half == w:
        return pltpu.roll(x, half, axis=1)
    lane = lax.broadcasted_iota(jnp.int32, x.shape, 1)
    return jnp.where(lane % (2 * half) < half,
                     pltpu.roll(x, w - half, axis=1), pltpu.roll(x, half, axis=1))


def _swa_kernel(sink_ref, q_ref, k_ref, v_ref, z_ref, cs_ref, sn_ref, kb_ref, vb_ref,
                y_ref, ko_ref, pk_scr, pv_scr, *, bb_n, lb, nb, hs, pos0):
    n = pl.program_id(1)

    @pl.when(n == 0)
    def _():
        pk_scr[...] = kb_ref[0]
        pv_scr[...] = vb_ref[0]

    cs = cs_ref[...]
    sn = sn_ref[...]
    cs_q = jnp.concatenate([cs] * (B_HEADS * B_HD // LANES), axis=1)
    sn_q = jnp.concatenate([sn] * (B_HEADS * B_HD // LANES), axis=1)
    rows = hs * lb
    r_idx = lax.broadcasted_iota(jnp.int32, (rows, WINDOW + lb), 0)
    a_idx = lax.rem(r_idx, lb)
    c_idx = lax.broadcasted_iota(jnp.int32, (rows, WINDOW + lb), 1)
    key_pos = pos0 + n * lb - WINDOW + c_idx
    mask = (c_idx >= a_idx) & (c_idx <= WINDOW + a_idx) & (key_pos >= 0)
    blk = lax.broadcasted_iota(jnp.int32, (rows, 1), 0) // lb
    sinks = []
    for h0 in range(0, B_HEADS, hs):
        col = jnp.full((rows, 1), sink_ref[h0], F32)
        for i in range(1, hs):
            col = jnp.where(blk == i, sink_ref[h0 + i], col)
        sinks.append(col)
    group = B_HEADS // B_KV_HEADS

    def per_b(bb, carry):
        q = q_ref[bb]
        q = q * cs_q + _rope_swap(q, B_HD // 2) * sn_q
        k = k_ref[bb]
        k = k * cs + _rope_swap(k, B_HD // 2) * sn
        v = v_ref[bb]
        ko_ref[bb] = k
        keys = jnp.concatenate([pk_scr[bb], k], axis=0).astype(BF16)
        vals = jnp.concatenate([pv_scr[bb], v], axis=0).astype(BF16)
        if nb > 1:
            pk_scr[bb] = k
            pv_scr[bb] = v
        scores = []
        for h0 in range(0, B_HEADS, hs):
            g = h0 // group
            qs = jnp.concatenate([q[:, h * B_HD:(h + 1) * B_HD] for h in range(h0, h0 + hs)],
                                 axis=0).astype(BF16)
            scores.append(lax.dot_general(qs, keys[:, g * B_HD:(g + 1) * B_HD], NT_DIMS,
                                          preferred_element_type=F32))
        probs = []
        for s, sink in zip(scores, sinks):
            s = jnp.where(mask, s * (B_HD ** -0.5), NEG_INF)
            m = jnp.maximum(jnp.max(s, axis=-1, keepdims=True), sink)
            p = jnp.exp(s - m)
            probs.append(p / (jnp.sum(p, axis=-1, keepdims=True) + jnp.exp(sink - m)))
        outs = []
        for i, p in enumerate(probs):
            g = (i * hs) // group
            o = jnp.dot(p.astype(BF16), vals[:, g * B_HD:(g + 1) * B_HD],
                        preferred_element_type=F32)
            outs.extend(o[j * lb:(j + 1) * lb] for j in range(hs))
        for j in range(B_HEADS // 2):
            sl = slice(j * LANES, (j + 1) * LANES)
            o = jnp.concatenate([outs[2 * j], outs[2 * j + 1]], axis=1)
            y_ref[bb, :, sl] = (o * z_ref[bb, :, sl]).astype(BF16)
        return carry

    lax.fori_loop(0, bb_n, per_b, 0)


def _swa(pb, z, k_buf, v_buf, layer, sinks, cos_t, sin_t, pos0):
    bn, ln, _ = pb.shape
    lb = WINDOW if ln % WINDOW == 0 else ln
    nb = ln // lb
    bb_n = 1 if nb > 1 else _pick_tile(bn, (8,))
    group = B_HEADS // B_KV_HEADS
    hs = group if group * lb <= LANES else 1
    qw = B_HEADS * B_HD
    tok = lambda n, j: pl.BlockSpec((bb_n, lb, n), lambda b, t: (b, t, j))
    buf = pl.BlockSpec((1, bb_n, WINDOW, LANES), lambda b, t: (layer, b, 0, 0))
    tab = pl.BlockSpec((lb, LANES), lambda b, t: (t, 0))
    return pl.pallas_call(
        functools.partial(_swa_kernel, bb_n=bb_n, lb=lb, nb=nb, hs=hs, pos0=pos0),
        grid=(bn // bb_n, nb),
        in_specs=[pl.BlockSpec(memory_space=pltpu.SMEM),
                  tok(qw, 0), tok(LANES, qw // LANES), tok(LANES, qw // LANES + 1),
                  tok(BRANCH_W, 1), tab, tab, buf, buf],
        out_specs=[tok(BRANCH_W, 0), tok(LANES, 0)],
        out_shape=[jax.ShapeDtypeStruct((bn, ln, BRANCH_W), BF16),
                   jax.ShapeDtypeStruct((bn, ln, LANES), F32)],
        scratch_shapes=[pltpu.VMEM((bb_n, WINDOW, LANES), F32),
                        pltpu.VMEM((bb_n, WINDOW, LANES), F32)],
        compiler_params=_params("parallel", "arbitrary"),
        name="swa",
    )(sinks, pb, pb, pb, z, cos_t, sin_t, k_buf, v_buf)


def _cols_to_rows(x, n_rows):
    r = lax.broadcasted_iota(jnp.int32, (n_rows, LANES), 0)
    c = lax.broadcasted_iota(jnp.int32, (n_rows, LANES), 1)
    sel = jnp.where(r == c, 1.0, 0.0).astype(BF16)
    hi = x.astype(BF16)
    rem = x - hi.astype(F32)
    mid = rem.astype(BF16)
    lo = (rem - mid.astype(F32)).astype(BF16)
    out = lax.dot_general(sel, hi, NT_DIMS, preferred_element_type=F32)
    out = out + lax.dot_general(sel, mid, NT_DIMS, preferred_element_type=F32)
    return out + lax.dot_general(sel, lo, NT_DIMS, preferred_element_type=F32)


def _mlstm_kernel(qk_ref, v_ref, gt_ref, z_ref, cb_ref, cw_ref, cbias_ref, gb_ref,
                  c0_ref, n0_ref, m0_ref, g_ref, b_ref,
                  y_ref, cout_ref, nout_ref, mout_ref,
                  c_scr, n_scr, m_scr, prev_scr, act_scr, gl_scr, *, bb_n, lt, lc):
    t = pl.program_id(1)

    @pl.when(t == 0)
    def _():
        c_scr[...] = c0_ref[0]
        n_scr[...] = n0_ref[0]
        m_scr[...] = m0_ref[0]
        prev_scr[...] = cb_ref[0]

    row8 = lax.broadcasted_iota(jnp.int32, (SUBLANES, C_CONV_COLS), 0)
    lane = lax.broadcasted_iota(jnp.int32, (lt, LANES), 1)
    qi = lax.broadcasted_iota(jnp.int32, (lc, lc), 0)
    ki = lax.broadcasted_iota(jnp.int32, (lc, lc), 1)
    tril = ki <= qi
    qkw = C_HEADS * C_QK

    def per_b(bb, carry):
        u = qk_ref[bb]
        prev8 = prev_scr[bb]
        conv = cbias_ref[...] + u * cw_ref[C_CONV - 1:C_CONV, :]
        for s in range(1, C_CONV):
            sh = pltpu.roll(u, s, axis=0)
            head = jnp.where(row8 < s, pltpu.roll(prev8, s, axis=0), sh[:SUBLANES])
            sh = head if lt == SUBLANES else jnp.concatenate([head, sh[SUBLANES:]], axis=0)
            conv = conv + sh * cw_ref[C_CONV - 1 - s:C_CONV - s, :]
        act_scr[...] = conv * jax.nn.sigmoid(conv)
        prev_scr[bb] = u[lt - SUBLANES:, :]
        g = gt_ref[bb] + gb_ref[...]
        gl_scr[...] = jnp.where(lane < C_HEADS, g, -_softplus(-g))

        def chunk(r0):
            glc = gl_scr[pl.ds(r0, lc), :]
            glr = _cols_to_rows(glc, 2 * C_HEADS)
            actc = act_scr[pl.ds(r0, lc), :]
            vc = v_ref[bb, pl.ds(r0, lc), :]
            zc = z_ref[bb, pl.ds(r0, lc), :]
            heads = range(C_HEADS)
            vsl = lambda h: slice(h * C_V, (h + 1) * C_V)
            qs, vs, qk, qc, kws, gate = [], [], [], [], [], []
            for h in heads:
                q = actc[:, h * C_QK:(h + 1) * C_QK]
                k = actc[:, qkw + h * C_QK:qkw + (h + 1) * C_QK] * (C_QK ** -0.5)
                qb = q.astype(BF16)
                qs.append(q)
                vs.append(vc[:, vsl(h)].astype(BF16))
                qk.append(lax.dot_general(qb, k.astype(BF16), NT_DIMS,
                                          preferred_element_type=F32))
                qc.append(jnp.dot(qb, c_scr[bb, h].astype(BF16), preferred_element_type=F32))
                li_col = glc[:, h:h + 1]
                lf_col = glc[:, C_HEADS + h:C_HEADS + h + 1]
                li_row = glr[h:h + 1, :]
                lf_row = glr[C_HEADS + h:C_HEADS + h + 1, :]
                b_col = jnp.sum(jnp.where(tril, lf_row, 0.0), axis=1, keepdims=True)
                b_row = jnp.sum(jnp.where(qi <= ki, lf_col, 0.0), axis=0, keepdims=True)
                m_s = m_scr[bb, h][:, :1]
                dmat = jnp.where(tril, b_col - b_row + li_row, NEG_INF)
                inter = b_col + m_s
                mt = jnp.maximum(inter, jnp.max(dmat, axis=1, keepdims=True))
                b_last = b_row[:, lc - 1:lc]
                gl_row = b_last - b_row + li_row
                gl_col = b_last - b_col + li_col
                m_new = jnp.maximum(b_last + m_s, jnp.max(gl_row, axis=1, keepdims=True))
                kws.append(k * jnp.exp(gl_col - m_new))
                gate.append((jnp.exp(dmat - mt), jnp.exp(inter - mt), mt, m_new,
                             jnp.exp(b_last + m_s - m_new)))
            nums, dens = [], []
            for h in heads:
                wts, sc, _, _, _ = gate[h]
                a = qk[h] * wts
                nums.append(jnp.dot(a.astype(BF16), vs[h], preferred_element_type=F32)
                            + sc * qc[h])
                dens.append(jnp.sum(a, axis=1, keepdims=True)
                            + sc * jnp.sum(qs[h] * n_scr[bb, h], axis=1, keepdims=True))
            for h in heads:
                _, _, _, m_new, s_old = gate[h]
                c_scr[bb, h] = s_old * c_scr[bb, h] + lax.dot_general(
                    kws[h].astype(BF16), vs[h], TN_DIMS, preferred_element_type=F32)
                n_scr[bb, h] = s_old * n_scr[bb, h] + jnp.sum(kws[h], axis=0, keepdims=True)
                m_scr[bb, h] = jnp.broadcast_to(m_new, (1, LANES))
            for h in heads:
                hh = nums[h] / jnp.maximum(jnp.abs(dens[h]), jnp.exp(-gate[h][2]))
                yn = _ln_lanes(hh, g_ref[:, vsl(h)], b_ref[:, vsl(h)])
                y_ref[bb, pl.ds(r0, lc), vsl(h)] = (yn * zc[:, vsl(h)]).astype(BF16)

        if lt == lc:
            chunk(0)
        else:
            def body(c, cc):
                chunk(pl.multiple_of(c * lc, lc))
                return cc
            lax.fori_loop(0, lt // lc, body, 0)
        return carry

    lax.fori_loop(0, bb_n, per_b, 0)

    @pl.when(t == pl.num_programs(1) - 1)
    def _():
        cout_ref[...] = c_scr[...]
        nout_ref[...] = n_scr[...]
        mout_ref[...] = m_scr[...]


def _mlstm(qk, v, gates, z, conv8, c0, n0, m0, layer, wts):
    bn, ln, _ = qk.shape
    lc = CHUNK if ln % CHUNK == 0 else ln
    lt = _pick_tile(ln, (256,))
    bb_n = 1 if ln > lt or bn < SUBLANES else SUBLANES
    tok = lambda n, j: pl.BlockSpec((bb_n, lt, n), lambda b, t: (b, t, j))
    st_in = lambda *s: pl.BlockSpec((1, bb_n) + s, lambda b, t: (layer, b) + (0,) * len(s))
    st_out = lambda *s: pl.BlockSpec((bb_n,) + s, lambda b, t: (b,) + (0,) * len(s))
    row = lambda n: _full((1, n))
    return pl.pallas_call(
        functools.partial(_mlstm_kernel, bb_n=bb_n, lt=lt, lc=lc),
        grid=(bn // bb_n, ln // lt),
        in_specs=[tok(C_CONV_COLS, 0), tok(BRANCH_W, 0), tok(LANES, 0), tok(BRANCH_W, 2),
                  st_in(SUBLANES, C_CONV_COLS), _full((C_CONV, C_CONV_COLS)), row(C_CONV_COLS),
                  row(LANES),
                  st_in(C_HEADS, C_QK, C_V), st_in(C_HEADS, 1, C_QK), st_in(C_HEADS, 1, LANES),
                  row(BRANCH_W), row(BRANCH_W)],
        out_specs=[tok(BRANCH_W, 0), st_out(C_HEADS, C_QK, C_V), st_out(C_HEADS, 1, C_QK),
                   st_out(C_HEADS, 1, LANES)],
        out_shape=[jax.ShapeDtypeStruct((bn, ln, BRANCH_W), BF16),
                   jax.ShapeDtypeStruct((bn, C_HEADS, C_QK, C_V), F32),
                   jax.ShapeDtypeStruct((bn, C_HEADS, 1, C_QK), F32),
                   jax.ShapeDtypeStruct((bn, C_HEADS, 1, LANES), F32)],
        scratch_shapes=[pltpu.VMEM((bb_n, C_HEADS, C_QK, C_V), F32),
                        pltpu.VMEM((bb_n, C_HEADS, 1, C_QK), F32),
                        pltpu.VMEM((bb_n, C_HEADS, 1, LANES), F32),
                        pltpu.VMEM((bb_n, SUBLANES, C_CONV_COLS), F32),
                        pltpu.VMEM((lt, C_CONV_COLS), F32),
                        pltpu.VMEM((lt, LANES), F32)],
        compiler_params=_params("parallel", "arbitrary"),
        name="mlstm",
    )(qk, v, gates, z, conv8, wts["c_conv_w"], wts["c_conv_b"], wts["c_gate_b"],
      c0, n0, m0, wts["c_ln_g"], wts["c_ln_b"])


def _ret_kernel(q_ref, k_ref, v_ref, z_ref, cs_ref, sn_ref, dm_ref, qd_ref, kd_ref, cd_ref,
                s0_ref, g_ref, b_ref, y_ref, sout_ref, s_scr, *, bb_n, lt, lc):
    t = pl.program_id(1)

    @pl.when(t == 0)
    def _():
        s_scr[...] = s0_ref[0]

    heads = range(D_HEADS)
    lanes_of = lambda h: slice(h * LANES, (h + 1) * LANES)

    def per_b(bb, carry):
        def chunk(r0):
            rows = pl.ds(r0, lc)
            cs = cs_ref[rows, :]
            sn = sn_ref[rows, :]
            qs, ks, vs, inner, cross = [], [], [], [], []
            for h in heads:
                q = q_ref[bb, rows, lanes_of(h)]
                k = k_ref[bb, rows, lanes_of(h)]
                q = (q * cs + _rope_swap(q, D_QK // 2) * sn).astype(BF16)
                k = (k * cs + _rope_swap(k, D_QK // 2) * sn) * (D_QK ** -0.5)
                qs.append(q)
                ks.append(k)
                vs.append(v_ref[bb, rows, lanes_of(h)].astype(BF16))
                inner.append(lax.dot_general(q, k.astype(BF16), NT_DIMS,
                                             preferred_element_type=F32))
                cross.append(jnp.dot(q, s_scr[bb, h].astype(BF16), preferred_element_type=F32))
            outs = []
            for h in heads:
                a = (inner[h] * dm_ref[h]).astype(BF16)
                outs.append(jnp.dot(a, vs[h], preferred_element_type=F32) + cross[h] * qd_ref[h])
            for h in heads:
                s_scr[bb, h] = cd_ref[h] * s_scr[bb, h] + lax.dot_general(
                    (ks[h] * kd_ref[h]).astype(BF16), vs[h], TN_DIMS, preferred_element_type=F32)
            for h in heads:
                yn = _ln_lanes(outs[h], g_ref[:, lanes_of(h)], b_ref[:, lanes_of(h)])
                y_ref[bb, rows, lanes_of(h)] = (yn * z_ref[bb, rows, lanes_of(h)]).astype(BF16)

        if lt == lc:
            chunk(0)
        else:
            def body(c, cc):
                chunk(pl.multiple_of(c * lc, lc))
                return cc
            lax.fori_loop(0, lt // lc, body, 0)
        return carry

    lax.fori_loop(0, bb_n, per_b, 0)

    @pl.when(t == pl.num_programs(1) - 1)
    def _():
        sout_ref[...] = s_scr[...]


def _retention(pd, z, s0, layer, wts, cos_t, sin_t):
    bn, ln, _ = pd.shape
    lc = CHUNK if ln % CHUNK == 0 else ln
    lt = _pick_tile(ln, (256,))
    bb_n = 1 if ln > lt or bn < SUBLANES else SUBLANES
    hw = D_HEADS * D_QK
    tok = lambda j: pl.BlockSpec((bb_n, lt, hw), lambda b, t: (b, t, j))
    tab = pl.BlockSpec((lt, LANES), lambda b, t: (t, 0))
    dec = wts["d_tables"][lc]
    return pl.pallas_call(
        functools.partial(_ret_kernel, bb_n=bb_n, lt=lt, lc=lc),
        grid=(bn // bb_n, ln // lt),
        in_specs=[tok(0), tok(1), tok(2), tok(3), tab, tab,
                  _full((D_HEADS, lc, lc)), _full((D_HEADS, lc, LANES)),
                  _full((D_HEADS, lc, LANES)), _full((D_HEADS, 1, LANES)),
                  pl.BlockSpec((1, bb_n, D_HEADS, D_QK, D_V), lambda b, t: (layer, b, 0, 0, 0)),
                  _full((1, BRANCH_W)), _full((1, BRANCH_W))],
        out_specs=[tok(0),
                   pl.BlockSpec((bb_n, D_HEADS, D_QK, D_V), lambda b, t: (b, 0, 0, 0))],
        out_shape=[jax.ShapeDtypeStruct((bn, ln, BRANCH_W), BF16),
                   jax.ShapeDtypeStruct((bn, D_HEADS, D_QK, D_V), F32)],
        scratch_shapes=[pltpu.VMEM((bb_n, D_HEADS, D_QK, D_V), F32)],
        compiler_params=_params("parallel", "arbitrary"),
        name="retention",
    )(pd, pd, pd, z, cos_t, sin_t, dec["decay_mat"], dec["q_dec"], dec["k_dec"], dec["c_dec"],
      s0, wts["d_ln_g"], wts["d_ln_b"])


def _rope_tables(pos, d):
    inv = ROPE_THETA ** (-jnp.arange(0, d, 2, dtype=F32) / d)
    ang = pos.astype(F32)[:, None] * inv[None, :]
    cos = jnp.cos(ang)
    sin = jnp.sin(ang)
    reps = LANES // d
    return (jnp.tile(jnp.concatenate([cos, cos], -1), (1, reps)),
            jnp.tile(jnp.concatenate([-sin, sin], -1), (1, reps)))


def _retention_tables(lc):
    log_gamma = jnp.log1p(-jnp.exp2(-5.0 - jnp.arange(D_HEADS, dtype=F32)))
    idx = jnp.arange(lc, dtype=F32)
    rel = idx[:, None] - idx[None, :]
    decay_mat = jnp.where(rel >= 0, jnp.exp(jnp.maximum(rel, 0.0) * log_gamma[:, None, None]), 0.0)
    q_dec = jnp.exp((idx + 1.0)[None, :] * log_gamma[:, None])
    k_dec = jnp.exp((lc - 1.0 - idx)[None, :] * log_gamma[:, None])
    c_dec = jnp.exp(lc * log_gamma)
    bcast = lambda a: jnp.broadcast_to(a[..., None], a.shape + (LANES,))
    return {"decay_mat": decay_mat, "q_dec": bcast(q_dec), "k_dec": bcast(k_dec),
            "c_dec": bcast(c_dec[:, None])}


def _layer_weights(l, w_in, a_mu, a_w0, a_w_up, a_a0, a_a_up, a_k_k, a_k_a, a_r_k, a_ln_g, a_ln_b,
                   b_sinks, c_conv_w, c_conv_b, c_i_bias, c_f_bias, c_ln_g, c_ln_b,
                   d_ln_g, d_ln_b, w_branch, w_out, ln_g, ln_b, chunk_lens):
    o1 = A_COLS
    o2 = o1 + B_COLS
    o3 = o2 + C_COLS
    o4 = MIX_COLS
    o5 = o4 + N_BRANCH * BRANCH_W
    og = o2 + C_CONV_COLS + C_HEADS * C_V
    wl = w_in[l]
    seg = lambda a, b: wl[:, a:b].astype(BF16)
    row = lambda a: a[l].reshape(1, -1)
    zeros = jnp.zeros((A_DECAY_LORA, BRANCH_W), F32)
    lora = jnp.concatenate([jnp.concatenate([a_w_up[l], zeros], 1),
                            jnp.concatenate([zeros, a_a_up[l]], 1)], 0).astype(BF16)
    gate_b = jnp.pad(jnp.concatenate([c_i_bias[l], c_f_bias[l]]), (0, LANES - 2 * C_HEADS))
    return {
        "w_a": seg(0, o1), "w_b": seg(o1, o2), "w_cqk": seg(o2, o2 + C_CONV_COLS),
        "w_cv": seg(o2 + C_CONV_COLS, og),
        "w_cg": jnp.pad(wl[:, og:o3], ((0, 0), (0, LANES - 2 * C_HEADS))).astype(BF16),
        "w_d": seg(o3, o4), "w_z": seg(o4, o5), "w_g": seg(o5, wl.shape[1]),
        "a_mu": row(a_mu), "a_w0": row(a_w0), "a_a0": row(a_a0), "a_lora": lora,
        "a_k_k": row(a_k_k), "a_k_a": row(a_k_a), "a_r_k": row(a_r_k),
        "a_ln_g": row(a_ln_g), "a_ln_b": row(a_ln_b),
        "b_sinks": b_sinks[l],
        "c_conv_w": c_conv_w[l], "c_conv_b": row(c_conv_b), "c_gate_b": gate_b.reshape(1, LANES),
        "c_ln_g": row(c_ln_g), "c_ln_b": row(c_ln_b),
        "d_ln_g": row(d_ln_g), "d_ln_b": row(d_ln_b),
        "d_tables": {lc: _retention_tables(lc) for lc in chunk_lens},
        "w_branch": w_branch[l].astype(BF16), "w_out": w_out[l].astype(BF16),
        "ln_g": row(ln_g), "ln_b": row(ln_b),
    }


def _hybrid_layer(xf, xb, pos0, l, st, wts, tabs):
    bn, ln, _ = xf.shape
    m = bn * ln
    s_a, shift_a, k_buf, v_buf, c_m, n_m, m_m, conv_m, s_d = st
    tok = lambda a: a.reshape(bn, ln, a.shape[-1])
    pa = tok(_matmul(xb, wts["w_a"], name="proj_a"))
    pb = tok(_matmul(xb, wts["w_b"], name="proj_b"))
    pcqk = tok(_matmul(xb, wts["w_cqk"], name="proj_cqk"))
    pcv = tok(_matmul(xb, wts["w_cv"], name="proj_cv"))
    pcg = tok(_matmul(xb, wts["w_cg"], name="proj_cg"))
    pd = tok(_matmul(xb, wts["w_d"], name="proj_d"))
    z = tok(_matmul(xb, wts["w_z"], act="silu", name="proj_z"))
    gate = _matmul(xb, wts["w_g"], act="sigmoid", name="proj_gate")

    y_a, s_a_new = _rwkv(pa, z, shift_a, s_a, l, wts)
    shift_new = pa[:, -1]

    y_b, k_rot = _swa(pb, z, k_buf, v_buf, l, wts["b_sinks"], tabs["cos64"], tabs["sin64"], pos0)
    qw = B_HEADS * B_HD
    kvw = B_KV_HEADS * B_HD
    kv_shape = (bn, WINDOW, B_KV_HEADS, B_HD)
    k_new = jnp.concatenate([k_buf[l], k_rot], 1)[:, -WINDOW:].reshape(kv_shape)
    v_new = jnp.concatenate([v_buf[l], pb[..., qw + kvw:]], 1)[:, -WINDOW:].reshape(kv_shape)

    y_c, c_new, n_new, m_new = _mlstm(pcqk, pcv, pcg, z, conv_m, c_m, n_m, m_m, l, wts)
    conv_new = jnp.concatenate([conv_m[l, :, SUBLANES - (C_CONV - 1):], pcqk], 1)[:, -(C_CONV - 1):]

    y_d, s_d_new = _retention(pd, z, s_d, l, wts, tabs["cos128"], tabs["sin128"])

    flat = lambda a: a.reshape(m, a.shape[-1])
    merged = _merge([flat(y_a), flat(y_b), flat(y_c), flat(y_d)], gate, wts["w_branch"])
    xf_new, xb_new = _out_ln(merged, wts["w_out"], flat(xf), wts["ln_g"], wts["ln_b"])
    new = (s_a_new, shift_new, k_new, v_new, c_new, n_new[:, :, 0], m_new[:, :, 0, 0],
           conv_new, s_d_new)
    return xf_new.reshape(bn, ln, D_MODEL), xb_new, new


def _trunk(x, pos0, states, layer_wts):
    bn, ln, _ = x.shape
    s_a, shift_a, k_buf, v_buf, c_m, n_m, m_m, conv_m, s_d = states
    depth = s_a.shape[0]
    st = (s_a,
          shift_a[:, :, None, :],
          k_buf.reshape(depth, bn, WINDOW, LANES),
          v_buf.reshape(depth, bn, WINDOW, LANES),
          c_m,
          n_m[:, :, :, None, :],
          jnp.broadcast_to(m_m[..., None, None], m_m.shape + (1, LANES)),
          jnp.pad(conv_m, ((0, 0), (0, 0), (SUBLANES - (C_CONV - 1), 0), (0, 0))),
          s_d)
    pos = pos0 + jnp.arange(ln)
    cos64, sin64 = _rope_tables(pos, B_HD)
    cos128, sin128 = _rope_tables(pos, D_QK)
    tabs = {"cos64": cos64, "sin64": sin64, "cos128": cos128, "sin128": sin128}
    xb = x.reshape(bn * ln, D_MODEL).astype(BF16)
    new = []
    for l in range(depth):
        x, xb, st_new = _hybrid_layer(x, xb, pos0, l, st, layer_wts[l], tabs)
        new.append(st_new)
    stacked = tuple(jnp.stack([s[j] for s in new]) for j in range(len(states)))
    return x, stacked


def kernel(x_prompt, x_sample, state_rwkv_S, state_rwkv_shift, cache_swa_k, cache_swa_v,
           state_mlstm_C, state_mlstm_n, state_mlstm_m, state_mlstm_conv, state_ret_S,
           w_in, a_mu, a_w0, a_w_up, a_a0, a_a_up, a_k_k, a_k_a, a_r_k, a_ln_g, a_ln_b,
           b_sinks, c_conv_w, c_conv_b, c_i_bias, c_f_bias, c_ln_g, c_ln_b,
           d_ln_g, d_ln_b, w_branch, w_out, ln_g, ln_b):
    weights = (w_in, a_mu, a_w0, a_w_up, a_a0, a_a_up, a_k_k, a_k_a, a_r_k, a_ln_g, a_ln_b,
               b_sinks, c_conv_w, c_conv_b, c_i_bias, c_f_bias, c_ln_g, c_ln_b,
               d_ln_g, d_ln_b, w_branch, w_out, ln_g, ln_b)
    sample_states = (state_rwkv_S, state_rwkv_shift, cache_swa_k, cache_swa_v,
                     state_mlstm_C, state_mlstm_n, state_mlstm_m, state_mlstm_conv, state_ret_S)
    depth = w_in.shape[0]
    chunk_len = lambda ln: CHUNK if ln % CHUNK == 0 else ln
    chunk_lens = {chunk_len(x_prompt.shape[1]), chunk_len(x_sample.shape[1])}
    layer_wts = [_layer_weights(l, *weights, chunk_lens) for l in range(depth)]
    n_prompt = x_prompt.shape[0]
    zero_states = tuple(jnp.zeros((depth, n_prompt) + s.shape[2:], s.dtype) for s in sample_states)
    y_prompt, p = _trunk(x_prompt, 0, zero_states, layer_wts)
    y_sample, s = _trunk(x_sample, PAST_LEN, sample_states, layer_wts)
    return (y_prompt, y_sample, p[0], s[0], p[1], s[1], p[2], s[2], p[3], s[3], p[4], s[4],
            p[5], s[5], p[6], s[6], p[7], s[7], p[8], s[8])
```

```python
import functools

import jax
import jax.numpy as jnp
from jax import lax
from jax.experimental import pallas as pl
from jax.experimental.pallas import tpu as pltpu

F32 = jnp.float32
BF16 = jnp.bfloat16

D_MODEL = 2048
DEPTH = 2
PAST_LEN = 8192
N_BRANCH = 4
BRANCH_W = D_MODEL // 2
A_HD = 64
A_HEADS = BRANCH_W // A_HD
A_DECAY_LORA = 64
A_AAA_LORA = 64
B_HD = 64
B_HEADS = BRANCH_W // B_HD
B_KV_HEADS = B_HEADS // 8
WINDOW = 128
ROPE_THETA = 10000.0
C_HEADS = 8
C_V = BRANCH_W // C_HEADS
C_QK = C_V // 2
C_CONV = 4
D_HEADS = 8
D_QK = BRANCH_W // D_HEADS
D_V = BRANCH_W // D_HEADS
CHUNK = 64
LN_EPS = 1e-5
NEG_INF = -1e30
ALPHA = (2 * DEPTH) ** 0.25

A_COLS = 3 * BRANCH_W + A_DECAY_LORA + A_AAA_LORA
B_COLS = (B_HEADS + 2 * B_KV_HEADS) * B_HD
C_CONV_COLS = 2 * C_HEADS * C_QK
C_COLS = C_CONV_COLS + C_HEADS * C_V + 2 * C_HEADS
D_COLS = D_HEADS * (2 * D_QK + D_V)
MIX_COLS = A_COLS + B_COLS + C_COLS + D_COLS

LANES = 128
SUBLANES = 8
VMEM_LIMIT_BYTES = 56 * 1024 * 1024

NT_DIMS = (((1,), (1,)), ((), ()))
TN_DIMS = (((0,), (0,)), ((), ()))


def _params(*sem):
    return pltpu.CompilerParams(dimension_semantics=sem, vmem_limit_bytes=VMEM_LIMIT_BYTES)


def _full(shape):
    n = len(shape)
    return pl.BlockSpec(shape, lambda *_: (0,) * n)


def _softplus(x):
    return jnp.maximum(x, 0.0) + jnp.log1p(jnp.exp(-jnp.abs(x)))


def _split2(x):
    hi = x.astype(BF16)
    lo = (x - hi.astype(F32)).astype(BF16)
    return hi, lo


def _pair_ones():
    r = lax.broadcasted_iota(jnp.int32, (LANES, LANES), 0) // A_HD
    c = lax.broadcasted_iota(jnp.int32, (LANES, LANES), 1) // A_HD
    return jnp.where(r == c, 1.0, 0.0).astype(BF16)


def _seg_sum(x, bd):
    hi, lo = _split2(x)
    return (jnp.dot(hi, bd, preferred_element_type=F32)
            + jnp.dot(lo, bd, preferred_element_type=F32))


def _ln_lanes(x, g, b):
    mu = jnp.mean(x, axis=-1, keepdims=True)
    d = x - mu
    var = jnp.mean(d * d, axis=-1, keepdims=True)
    return d * lax.rsqrt(var + LN_EPS) * g + b


def _mm_kernel(x_ref, w_ref, o_ref, *, act):
    acc = jnp.dot(x_ref[...], w_ref[...], preferred_element_type=F32)
    if act == "silu":
        acc = acc * jax.nn.sigmoid(acc)
    elif act == "sigmoid":
        acc = jax.nn.sigmoid(acc)
    o_ref[...] = acc.astype(o_ref.dtype)


def _pick_tile(n, candidates):
    for c in candidates:
        if n % c == 0:
            return c
    return n


def _matmul(x, w, act=None, name="proj"):
    m, k = x.shape
    n = w.shape[1]
    tm = _pick_tile(m, (1024,))
    tn = _pick_tile(n, (1024, 640, 512, 256, 128))
    return pl.pallas_call(
        functools.partial(_mm_kernel, act=act),
        grid=(m // tm, n // tn),
        in_specs=[pl.BlockSpec((tm, k), lambda i, j: (i, 0)),
                  pl.BlockSpec((k, tn), lambda i, j: (0, j))],
        out_specs=pl.BlockSpec((tm, tn), lambda i, j: (i, j)),
        out_shape=jax.ShapeDtypeStruct((m, n), F32),
        compiler_params=_params("parallel", "arbitrary"),
        name=name,
    )(x, w)


def _merge_kernel(ya_ref, yb_ref, yc_ref, yd_ref, g_ref, w_ref, o_ref, acc_ref):
    i = pl.program_id(1)

    @pl.when(i == 0)
    def _():
        acc_ref[...] = jnp.zeros_like(acc_ref)

    for b, y_ref in enumerate((ya_ref, yb_ref, yc_ref, yd_ref)):
        @pl.when(i == b)
        def _(y_ref=y_ref):
            acc_ref[...] += g_ref[...] * jnp.dot(y_ref[...], w_ref[0],
                                                 preferred_element_type=F32)

    @pl.when(i == N_BRANCH - 1)
    def _():
        o_ref[...] = acc_ref[...].astype(o_ref.dtype)


def _merge(ys, gate, w_branch):
    m = gate.shape[0]
    tm = _pick_tile(m, (512,))
    y_spec = pl.BlockSpec((tm, BRANCH_W), lambda i, j: (i, 0))
    return pl.pallas_call(
        _merge_kernel,
        grid=(m // tm, N_BRANCH),
        in_specs=[y_spec, y_spec, y_spec, y_spec,
                  pl.BlockSpec((tm, D_MODEL), lambda i, j: (i, j)),
                  pl.BlockSpec((1, BRANCH_W, D_MODEL), lambda i, j: (j, 0, 0))],
        out_specs=pl.BlockSpec((tm, D_MODEL), lambda i, j: (i, 0)),
        out_shape=jax.ShapeDtypeStruct((m, D_MODEL), BF16),
        scratch_shapes=[pltpu.VMEM((tm, D_MODEL), F32)],
        compiler_params=_params("parallel", "arbitrary"),
        name="merge",
    )(*ys, gate, w_branch)


def _outln_kernel(m_ref, w_ref, x_ref, g_ref, b_ref, of_ref, ob_ref):
    out = jnp.dot(m_ref[...], w_ref[...], preferred_element_type=F32)
    y = _ln_lanes(ALPHA * x_ref[...] + out, g_ref[...], b_ref[...])
    of_ref[...] = y
    ob_ref[...] = y.astype(BF16)


def _out_ln(merged, w_out, x, g, b):
    m = x.shape[0]
    tm = _pick_tile(m, (512,))
    row = pl.BlockSpec((tm, D_MODEL), lambda i: (i, 0))
    return pl.pallas_call(
        _outln_kernel,
        grid=(m // tm,),
        in_specs=[row, _full((D_MODEL, D_MODEL)), row, _full((1, D_MODEL)), _full((1, D_MODEL))],
        out_specs=[row, row],
        out_shape=[jax.ShapeDtypeStruct((m, D_MODEL), F32),
                   jax.ShapeDtypeStruct((m, D_MODEL), BF16)],
        compiler_params=_params("parallel"),
        name="out_ln",
    )(merged, w_out, x, g, b)


N_PAIR = A_HEADS // 2
N_QUAD = N_PAIR // 2
CHAINS_PER_DOT = 4


def _quad_ones():
    n = 2 * LANES
    r = lax.broadcasted_iota(jnp.int32, (n, n), 0) // A_HD
    c = lax.broadcasted_iota(jnp.int32, (n, n), 1) // A_HD
    return jnp.where(r == c, 1.0, 0.0).astype(BF16)


def _rwkv_kernel(pa_ref, sh_ref, s0_ref, z_ref, mu_ref, w0_ref, a0_ref, wl_ref, kk_ref, ka_ref,
                 rk_ref, g_ref, b_ref, y_ref, sout_ref,
                 s_scr, sb_scr, prev_scr, r_s, w_s, k_s, v_s, kap_s, beta_s, y_s, *, bb_n, lt):
    t = pl.program_id(1)
    bd = _pair_ones()
    bd2 = _quad_ones()

    @pl.when(t == 0)
    def _():
        prev_scr[...] = sh_ref[0]
        for bb in range(bb_n):
            for p in range(N_PAIR):
                s = jnp.concatenate([s0_ref[0, bb, 2 * p], s0_ref[0, bb, 2 * p + 1]], axis=1)
                s_scr[bb, p] = s
                sb_scr[bb, p] = s.astype(BF16)

    w3 = 3 * BRANCH_W
    for bb in range(bb_n):
        pa = pa_ref[bb]
        row = lax.broadcasted_iota(jnp.int32, pa.shape, 0)
        prev = jnp.where(row == 0, prev_scr[bb], pltpu.roll(pa, 1, axis=0))
        prev_scr[bb] = pa[lt - 1:lt, :]
        xs = pa + (prev - pa) * mu_ref[...]
        lo = xs[:, w3:]
        lane = lax.broadcasted_iota(jnp.int32, lo.shape, 1)
        lo = jnp.where(lane < A_DECAY_LORA, jnp.tanh(lo), lo)
        pre = jnp.dot(lo.astype(BF16), wl_ref[...], preferred_element_type=F32)
        for p in range(N_PAIR):
            sl = slice(p * LANES, (p + 1) * LANES)
            r = xs[:, sl]
            k = xs[:, BRANCH_W + p * LANES:BRANCH_W + (p + 1) * LANES]
            v = xs[:, 2 * BRANCH_W + p * LANES:2 * BRANCH_W + (p + 1) * LANES]
            w_log = -_softplus(-(w0_ref[:, sl] + pre[:, sl])) - 0.5
            a = jax.nn.sigmoid(a0_ref[:, sl]
                               + pre[:, BRANCH_W + p * LANES:BRANCH_W + (p + 1) * LANES])
            kk = k * kk_ref[:, sl]
            kk = kk / jnp.maximum(jnp.sqrt(_seg_sum(kk * kk, bd)), 1e-12)
            r_s[bb, :, sl] = r
            w_s[bb, :, sl] = jnp.exp(-jnp.exp(w_log))
            k_s[bb, :, sl] = k * (1.0 + (a - 1.0) * ka_ref[:, sl])
            v_s[bb, :, sl] = v
            kap_s[bb, :, sl] = kk
            beta_s[bb, :, sl] = kk * a

    vi = lax.broadcasted_iota(jnp.int32, (A_HD, LANES), 0)
    li = lax.broadcasted_iota(jnp.int32, (A_HD, LANES), 1)
    diag = jnp.where((li % A_HD) == vi, 1.0, 0.0)
    diag_bf = diag.astype(BF16)
    row8 = lax.broadcasted_iota(jnp.int32, (SUBLANES, LANES), 0)
    chains = [(bb, q) for bb in range(bb_n) for q in range(N_QUAD)]
    groups = [chains[i:i + CHAINS_PER_DOT] for i in range(0, len(chains), CHAINS_PER_DOT)]
    lanes_of = lambda p: slice(p * LANES, (p + 1) * LANES)
    two = 2 * A_HD

    def steps(c, carry):
        rows = pl.ds(pl.multiple_of(c * SUBLANES, SUBLANES), SUBLANES)
        ytile = {(bb, p): jnp.zeros((SUBLANES, LANES), F32)
                 for bb in range(bb_n) for p in range(N_PAIR)}
        for j in range(SUBLANES):
            rowj = lambda ref, bb, p: ref[bb, rows, lanes_of(p)][j:j + 1]
            rowj_bf = lambda ref, bb, p: jnp.broadcast_to(
                rowj(ref, bb, p), (A_HD, LANES)).astype(BF16)
            res1 = []
            for grp in groups:
                lhs = []
                for bb, q in grp:
                    ps, vds = [], []
                    for p in (2 * q, 2 * q + 1):
                        ps.append(sb_scr[bb, p] * rowj_bf(kap_s, bb, p))
                        vds.append(diag_bf * rowj_bf(v_s, bb, p))
                    lhs += [jnp.concatenate(ps, axis=1), jnp.concatenate(vds, axis=1)]
                res1.append(jnp.dot(jnp.concatenate(lhs, axis=0), bd2,
                                    preferred_element_type=F32))
            res2 = []
            for grp, res in zip(groups, res1):
                lhs = []
                for c_i, (bb, q) in enumerate(grp):
                    qs = []
                    for i, p in enumerate((2 * q, 2 * q + 1)):
                        u = res[c_i * two:c_i * two + A_HD, lanes_of(i)]
                        vb = res[c_i * two + A_HD:(c_i + 1) * two, lanes_of(i)]
                        s = (s_scr[bb, p] * rowj(w_s, bb, p) - u * rowj(beta_s, bb, p)
                             + vb * rowj(k_s, bb, p))
                        s_scr[bb, p] = s
                        sb = s.astype(BF16)
                        sb_scr[bb, p] = sb
                        qs.append(sb * rowj_bf(r_s, bb, p))
                    lhs.append(jnp.concatenate(qs, axis=1))
                res2.append(jnp.dot(jnp.concatenate(lhs, axis=0), bd2,
                                    preferred_element_type=F32))
            for grp, res in zip(groups, res2):
                for c_i, (bb, q) in enumerate(grp):
                    for i, p in enumerate((2 * q, 2 * q + 1)):
                        yb = res[c_i * A_HD:(c_i + 1) * A_HD, lanes_of(i)]
                        y_row = jnp.sum(yb * diag, axis=0, keepdims=True)
                        ytile[bb, p] = jnp.where(row8 == j, y_row, ytile[bb, p])
        for (bb, p), tile in ytile.items():
            y_s[bb, rows, lanes_of(p)] = tile
        return carry

    lax.fori_loop(0, lt // SUBLANES, steps, 0)

    inv_hd = 1.0 / A_HD
    for bb in range(bb_n):
        for p in range(N_PAIR):
            sl = lanes_of(p)
            y = y_s[bb, :, sl]
            d = y - _seg_sum(y, bd) * inv_hd
            var = _seg_sum(d * d, bd) * inv_hd
            yn = d * lax.rsqrt(var + LN_EPS) * g_ref[:, sl] + b_ref[:, sl]
            bonus = _seg_sum(r_s[bb, :, sl] * k_s[bb, :, sl] * rk_ref[:, sl], bd) * v_s[bb, :, sl]
            y_ref[bb, :, sl] = ((yn + bonus) * z_ref[bb, :, sl]).astype(BF16)

    @pl.when(t == pl.num_programs(1) - 1)
    def _():
        for bb in range(bb_n):
            for p in range(N_PAIR):
                s = s_scr[bb, p]
                sout_ref[bb, 2 * p] = s[:, :A_HD]
                sout_ref[bb, 2 * p + 1] = s[:, A_HD:]


def _rwkv(pa, z, shift, s0, layer, wts):
    bn, ln, _ = pa.shape
    lt = _pick_tile(ln, (256,))
    bb_n = _pick_tile(bn, (2,) if ln > lt else (4, 2))
    row = lambda n: _full((1, n))
    tok = lambda n, j: pl.BlockSpec((bb_n, lt, n), lambda b, t: (b, t, j))
    scr = lambda: pltpu.VMEM((bb_n, lt, BRANCH_W), F32)
    return pl.pallas_call(
        functools.partial(_rwkv_kernel, bb_n=bb_n, lt=lt),
        grid=(bn // bb_n, ln // lt),
        in_specs=[tok(A_COLS, 0),
                  pl.BlockSpec((1, bb_n, 1, A_COLS), lambda b, t: (layer, b, 0, 0)),
                  pl.BlockSpec((1, bb_n, A_HEADS, A_HD, A_HD), lambda b, t: (layer, b, 0, 0, 0)),
                  tok(BRANCH_W, 0),
                  row(A_COLS), row(BRANCH_W), row(BRANCH_W), _full((LANES, 2 * BRANCH_W)),
                  row(BRANCH_W), row(BRANCH_W), row(BRANCH_W), row(BRANCH_W), row(BRANCH_W)],
        out_specs=[tok(BRANCH_W, 0),
                   pl.BlockSpec((bb_n, A_HEADS, A_HD, A_HD), lambda b, t: (b, 0, 0, 0))],
        out_shape=[jax.ShapeDtypeStruct((bn, ln, BRANCH_W), BF16),
                   jax.ShapeDtypeStruct((bn, A_HEADS, A_HD, A_HD), F32)],
        scratch_shapes=[pltpu.VMEM((bb_n, N_PAIR, A_HD, LANES), F32),
                        pltpu.VMEM((bb_n, N_PAIR, A_HD, LANES), BF16),
                        pltpu.VMEM((bb_n, 1, A_COLS), F32),
                        scr(), scr(), scr(), scr(), scr(), scr(), scr()],
        compiler_params=_params("parallel", "arbitrary"),
        name="rwkv7",
    )(pa, shift, s0, z, wts["a_mu"], wts["a_w0"], wts["a_a0"], wts["a_lora"], wts["a_k_k"],
      wts["a_k_a"], wts["a_r_k"], wts["a_ln_g"], wts["a_ln_b"])


def _rope_swap(x, half):
    w = x.shape[-1]
    if 2 * half == w:
        return pltpu.roll(x, half, axis=1)
    lane = lax.broadcasted_iota(jnp.int32, x.shape, 1)
    return jnp.where(lane % (2 * half) < half,
                     pltpu.roll(x, w - half, axis=1), pltpu.roll(x, half, axis=1))


def _swa_kernel(sink_ref, q_ref, k_ref, v_ref, z_ref, cs_ref, sn_ref, kb_ref, vb_ref,
                y_ref, ko_ref, pk_scr, pv_scr, *, bb_n, lb, nb, hs, pos0):
    n = pl.program_id(1)

    @pl.when(n == 0)
    def _():
        pk_scr[...] = kb_ref[0]
        pv_scr[...] = vb_ref[0]

    cs = cs_ref[...]
    sn = sn_ref[...]
    cs_q = jnp.concatenate([cs] * (B_HEADS * B_HD // LANES), axis=1)
    sn_q = jnp.concatenate([sn] * (B_HEADS * B_HD // LANES), axis=1)
    rows = hs * lb
    r_idx = lax.broadcasted_iota(jnp.int32, (rows, WINDOW + lb), 0)
    a_idx = lax.rem(r_idx, lb)
    c_idx = lax.broadcasted_iota(jnp.int32, (rows, WINDOW + lb), 1)
    key_pos = pos0 + n * lb - WINDOW + c_idx
    mask = (c_idx >= a_idx) & (c_idx <= WINDOW + a_idx) & (key_pos >= 0)
    blk = lax.broadcasted_iota(jnp.int32, (rows, 1), 0) // lb
    sinks = []
    for h0 in range(0, B_HEADS, hs):
        col = jnp.full((rows, 1), sink_ref[h0], F32)
        for i in range(1, hs):
            col = jnp.where(blk == i, sink_ref[h0 + i], col)
        sinks.append(col)
    group = B_HEADS // B_KV_HEADS

    def per_b(bb, carry):
        q = q_ref[bb]
        q = q * cs_q + _rope_swap(q, B_HD // 2) * sn_q
        k = k_ref[bb]
        k = k * cs + _rope_swap(k, B_HD // 2) * sn
        v = v_ref[bb]
        ko_ref[bb] = k
        keys = jnp.concatenate([pk_scr[bb], k], axis=0).astype(BF16)
        vals = jnp.concatenate([pv_scr[bb], v], axis=0).astype(BF16)
        if nb > 1:
            pk_scr[bb] = k
            pv_scr[bb] = v
        scores = []
        for h0 in range(0, B_HEADS, hs):
            g = h0 // group
            qs = jnp.concatenate([q[:, h * B_HD:(h + 1) * B_HD] for h in range(h0, h0 + hs)],
                                 axis=0).astype(BF16)
            scores.append(lax.dot_general(qs, keys[:, g * B_HD:(g + 1) * B_HD], NT_DIMS,
                                          preferred_element_type=F32))
        probs = []
        for s, sink in zip(scores, sinks):
            s = jnp.where(mask, s * (B_HD ** -0.5), NEG_INF)
            m = jnp.maximum(jnp.max(s, axis=-1, keepdims=True), sink)
            p = jnp.exp(s - m)
            probs.append(p / (jnp.sum(p, axis=-1, keepdims=True) + jnp.exp(sink - m)))
        outs = []
        for i, p in enumerate(probs):
            g = (i * hs) // group
            o = jnp.dot(p.astype(BF16), vals[:, g * B_HD:(g + 1) * B_HD],
                        preferred_element_type=F32)
            outs.extend(o[j * lb:(j + 1) * lb] for j in range(hs))
        for j in range(B_HEADS // 2):
            sl = slice(j * LANES, (j + 1) * LANES)
            o = jnp.concatenate([outs[2 * j], outs[2 * j + 1]], axis=1)
            y_ref[bb, :, sl] = (o * z_ref[bb, :, sl]).astype(BF16)
        return carry

    lax.fori_loop(0, bb_n, per_b, 0)


def _swa(pb, z, k_buf, v_buf, layer, sinks, cos_t, sin_t, pos0):
    bn, ln, _ = pb.shape
    lb = WINDOW if ln % WINDOW == 0 else ln
    nb = ln // lb
    bb_n = 1 if nb > 1 else _pick_tile(bn, (8,))
    group = B_HEADS // B_KV_HEADS
    hs = group if group * lb <= LANES else 1
    qw = B_HEADS * B_HD
    tok = lambda n, j: pl.BlockSpec((bb_n, lb, n), lambda b, t: (b, t, j))
    buf = pl.BlockSpec((1, bb_n, WINDOW, LANES), lambda b, t: (layer, b, 0, 0))
    tab = pl.BlockSpec((lb, LANES), lambda b, t: (t, 0))
    return pl.pallas_call(
        functools.partial(_swa_kernel, bb_n=bb_n, lb=lb, nb=nb, hs=hs, pos0=pos0),
        grid=(bn // bb_n, nb),
        in_specs=[pl.BlockSpec(memory_space=pltpu.SMEM),
                  tok(qw, 0), tok(LANES, qw // LANES), tok(LANES, qw // LANES + 1),
                  tok(BRANCH_W, 1), tab, tab, buf, buf],
        out_specs=[tok(BRANCH_W, 0), tok(LANES, 0)],
        out_shape=[jax.ShapeDtypeStruct((bn, ln, BRANCH_W), BF16),
                   jax.ShapeDtypeStruct((bn, ln, LANES), F32)],
        scratch_shapes=[pltpu.VMEM((bb_n, WINDOW, LANES), F32),
                        pltpu.VMEM((bb_n, WINDOW, LANES), F32)],
        compiler_params=_params("parallel", "arbitrary"),
        name="swa",
    )(sinks, pb, pb, pb, z, cos_t, sin_t, k_buf, v_buf)


def _cols_to_rows(x, n_rows):
    r = lax.broadcasted_iota(jnp.int32, (n_rows, LANES), 0)
    c = lax.broadcasted_iota(jnp.int32, (n_rows, LANES), 1)
    sel = jnp.where(r == c, 1.0, 0.0).astype(BF16)
    hi = x.astype(BF16)
    rem = x - hi.astype(F32)
    mid = rem.astype(BF16)
    lo = (rem - mid.astype(F32)).astype(BF16)
    out = lax.dot_general(sel, hi, NT_DIMS, preferred_element_type=F32)
    out = out + lax.dot_general(sel, mid, NT_DIMS, preferred_element_type=F32)
    return out + lax.dot_general(sel, lo, NT_DIMS, preferred_element_type=F32)


def _mlstm_kernel(qk_ref, v_ref, gt_ref, z_ref, cb_ref, cw_ref, cbias_ref, gb_ref,
                  c0_ref, n0_ref, m0_ref, g_ref, b_ref,
                  y_ref, cout_ref, nout_ref, mout_ref,
                  c_scr, n_scr, m_scr, prev_scr, act_scr, gl_scr, *, bb_n, lt, lc):
    t = pl.program_id(1)

    @pl.when(t == 0)
    def _():
        c_scr[...] = c0_ref[0]
        n_scr[...] = n0_ref[0]
        m_scr[...] = m0_ref[0]
        prev_scr[...] = cb_ref[0]

    row8 = lax.broadcasted_iota(jnp.int32, (SUBLANES, C_CONV_COLS), 0)
    lane = lax.broadcasted_iota(jnp.int32, (lt, LANES), 1)
    qi = lax.broadcasted_iota(jnp.int32, (lc, lc), 0)
    ki = lax.broadcasted_iota(jnp.int32, (lc, lc), 1)
    tril = ki <= qi
    qkw = C_HEADS * C_QK

    def per_b(bb, carry):
        u = qk_ref[bb]
        prev8 = prev_scr[bb]
        conv = cbias_ref[...] + u * cw_ref[C_CONV - 1:C_CONV, :]
        for s in range(1, C_CONV):
            sh = pltpu.roll(u, s, axis=0)
            head = jnp.where(row8 < s, pltpu.roll(prev8, s, axis=0), sh[:SUBLANES])
            sh = head if lt == SUBLANES else jnp.concatenate([head, sh[SUBLANES:]], axis=0)
            conv = conv + sh * cw_ref[C_CONV - 1 - s:C_CONV - s, :]
        act_scr[...] = conv * jax.nn.sigmoid(conv)
        prev_scr[bb] = u[lt - SUBLANES:, :]
        g = gt_ref[bb] + gb_ref[...]
        gl_scr[...] = jnp.where(lane < C_HEADS, g, -_softplus(-g))

        def chunk(r0):
            glc = gl_scr[pl.ds(r0, lc), :]
            glr = _cols_to_rows(glc, 2 * C_HEADS)
            actc = act_scr[pl.ds(r0, lc), :]
            vc = v_ref[bb, pl.ds(r0, lc), :]
            zc = z_ref[bb, pl.ds(r0, lc), :]
            heads = range(C_HEADS)
            vsl = lambda h: slice(h * C_V, (h + 1) * C_V)
            qs, vs, qk, qc, kws, gate = [], [], [], [], [], []
            for h in heads:
                q = actc[:, h * C_QK:(h + 1) * C_QK]
                k = actc[:, qkw + h * C_QK:qkw + (h + 1) * C_QK] * (C_QK ** -0.5)
                qb = q.astype(BF16)
                qs.append(q)
                vs.append(vc[:, vsl(h)].astype(BF16))
                qk.append(lax.dot_general(qb, k.astype(BF16), NT_DIMS,
                                          preferred_element_type=F32))
                qc.append(jnp.dot(qb, c_scr[bb, h].astype(BF16), preferred_element_type=F32))
                li_col = glc[:, h:h + 1]
                lf_col = glc[:, C_HEADS + h:C_HEADS + h + 1]
                li_row = glr[h:h + 1, :]
                lf_row = glr[C_HEADS + h:C_HEADS + h + 1, :]
                b_col = jnp.sum(jnp.where(tril, lf_row, 0.0), axis=1, keepdims=True)
                b_row = jnp.sum(jnp.where(qi <= ki, lf_col, 0.0), axis=0, keepdims=True)
                m_s = m_scr[bb, h][:, :1]
                dmat = jnp.where(tril, b_col - b_row + li_row, NEG_INF)
                inter = b_col + m_s
                mt = jnp.maximum(inter, jnp.max(dmat, axis=1, keepdims=True))
                b_last = b_row[:, lc - 1:lc]
                gl_row = b_last - b_row + li_row
                gl_col = b_last - b_col + li_col
                m_new = jnp.maximum(b_last + m_s, jnp.max(gl_row, axis=1, keepdims=True))
                kws.append(k * jnp.exp(gl_col - m_new))
                gate.append((jnp.exp(dmat - mt), jnp.exp(inter - mt), mt, m_new,
                             jnp.exp(b_last + m_s - m_new)))
            nums, dens = [], []
            for h in heads:
                wts, sc, _, _, _ = gate[h]
                a = qk[h] * wts
                nums.append(jnp.dot(a.astype(BF16), vs[h], preferred_element_type=F32)
                            + sc * qc[h])
                dens.append(jnp.sum(a, axis=1, keepdims=True)
                            + sc * jnp.sum(qs[h] * n_scr[bb, h], axis=1, keepdims=True))
            for h in heads:
                _, _, _, m_new, s_old = gate[h]
                c_scr[bb, h] = s_old * c_scr[bb, h] + lax.dot_general(
                    kws[h].astype(BF16), vs[h], TN_DIMS, preferred_element_type=F32)
                n_scr[bb, h] = s_old * n_scr[bb, h] + jnp.sum(kws[h], axis=0, keepdims=True)
                m_scr[bb, h] = jnp.broadcast_to(m_new, (1, LANES))
            for h in heads:
                hh = nums[h] / jnp.maximum(jnp.abs(dens[h]), jnp.exp(-gate[h][2]))
                yn = _ln_lanes(hh, g_ref[:, vsl(h)], b_ref[:, vsl(h)])
                y_ref[bb, pl.ds(r0, lc), vsl(h)] = (yn * zc[:, vsl(h)]).astype(BF16)

        if lt == lc:
            chunk(0)
        else:
            def body(c, cc):
                chunk(pl.multiple_of(c * lc, lc))
                return cc
            lax.fori_loop(0, lt // lc, body, 0)
        return carry

    lax.fori_loop(0, bb_n, per_b, 0)

    @pl.when(t == pl.num_programs(1) - 1)
    def _():
        cout_ref[...] = c_scr[...]
        nout_ref[...] = n_scr[...]
        mout_ref[...] = m_scr[...]


def _mlstm(qk, v, gates, z, conv8, c0, n0, m0, layer, wts):
    bn, ln, _ = qk.shape
    lc = CHUNK if ln % CHUNK == 0 else ln
    lt = _pick_tile(ln, (256,))
    bb_n = 1 if ln > lt or bn < SUBLANES else SUBLANES
    tok = lambda n, j: pl.BlockSpec((bb_n, lt, n), lambda b, t: (b, t, j))
    st_in = lambda *s: pl.BlockSpec((1, bb_n) + s, lambda b, t: (layer, b) + (0,) * len(s))
    st_out = lambda *s: pl.BlockSpec((bb_n,) + s, lambda b, t: (b,) + (0,) * len(s))
    row = lambda n: _full((1, n))
    return pl.pallas_call(
        functools.partial(_mlstm_kernel, bb_n=bb_n, lt=lt, lc=lc),
        grid=(bn // bb_n, ln // lt),
        in_specs=[tok(C_CONV_COLS, 0), tok(BRANCH_W, 0), tok(LANES, 0), tok(BRANCH_W, 2),
                  st_in(SUBLANES, C_CONV_COLS), _full((C_CONV, C_CONV_COLS)), row(C_CONV_COLS),
                  row(LANES),
                  st_in(C_HEADS, C_QK, C_V), st_in(C_HEADS, 1, C_QK), st_in(C_HEADS, 1, LANES),
                  row(BRANCH_W), row(BRANCH_W)],
        out_specs=[tok(BRANCH_W, 0), st_out(C_HEADS, C_QK, C_V), st_out(C_HEADS, 1, C_QK),
                   st_out(C_HEADS, 1, LANES)],
        out_shape=[jax.ShapeDtypeStruct((bn, ln, BRANCH_W), BF16),
                   jax.ShapeDtypeStruct((bn, C_HEADS, C_QK, C_V), F32),
                   jax.ShapeDtypeStruct((bn, C_HEADS, 1, C_QK), F32),
                   jax.ShapeDtypeStruct((bn, C_HEADS, 1, LANES), F32)],
        scratch_shapes=[pltpu.VMEM((bb_n, C_HEADS, C_QK, C_V), F32),
                        pltpu.VMEM((bb_n, C_HEADS, 1, C_QK), F32),
                        pltpu.VMEM((bb_n, C_HEADS, 1, LANES), F32),
                        pltpu.VMEM((bb_n, SUBLANES, C_CONV_COLS), F32),
                        pltpu.VMEM((lt, C_CONV_COLS), F32),
                        pltpu.VMEM((lt, LANES), F32)],
        compiler_params=_params("parallel", "arbitrary"),
        name="mlstm",
    )(qk, v, gates, z, conv8, wts["c_conv_w"], wts["c_conv_b"], wts["c_gate_b"],
      c0, n0, m0, wts["c_ln_g"], wts["c_ln_b"])


def _ret_kernel(q_ref, k_ref, v_ref, z_ref, cs_ref, sn_ref, dm_ref, qd_ref, kd_ref, cd_ref,
                s0_ref, g_ref, b_ref, y_ref, sout_ref, s_scr, *, bb_n, lt, lc):
    t = pl.program_id(1)

    @pl.when(t == 0)
    def _():
        s_scr[...] = s0_ref[0]

    heads = range(D_HEADS)
    lanes_of = lambda h: slice(h * LANES, (h + 1) * LANES)

    def per_b(bb, carry):
        def chunk(r0):
            rows = pl.ds(r0, lc)
            cs = cs_ref[rows, :]
            sn = sn_ref[rows, :]
            qs, ks, vs, inner, cross = [], [], [], [], []
            for h in heads:
                q = q_ref[bb, rows, lanes_of(h)]
                k = k_ref[bb, rows, lanes_of(h)]
                q = (q * cs + _rope_swap(q, D_QK // 2) * sn).astype(BF16)
                k = (k * cs + _rope_swap(k, D_QK // 2) * sn) * (D_QK ** -0.5)
                qs.append(q)
                ks.append(k)
                vs.append(v_ref[bb, rows, lanes_of(h)].astype(BF16))
                inner.append(lax.dot_general(q, k.astype(BF16), NT_DIMS,
                                             preferred_element_type=F32))
                cross.append(jnp.dot(q, s_scr[bb, h].astype(BF16), preferred_element_type=F32))
            outs = []
            for h in heads:
                a = (inner[h] * dm_ref[h]).astype(BF16)
                outs.append(jnp.dot(a, vs[h], preferred_element_type=F32) + cross[h] * qd_ref[h])
            for h in heads:
                s_scr[bb, h] = cd_ref[h] * s_scr[bb, h] + lax.dot_general(
                    (ks[h] * kd_ref[h]).astype(BF16), vs[h], TN_DIMS, preferred_element_type=F32)
            for h in heads:
                yn = _ln_lanes(outs[h], g_ref[:, lanes_of(h)], b_ref[:, lanes_of(h)])
                y_ref[bb, rows, lanes_of(h)] = (yn * z_ref[bb, rows, lanes_of(h)]).astype(BF16)

        if lt == lc:
            chunk(0)
        else:
            def body(c, cc):
                chunk(pl.multiple_of(c * lc, lc))
                return cc
            lax.fori_loop(0, lt // lc, body, 0)
        return carry

    lax.fori_loop(0, bb_n, per_b, 0)

    @pl.when(t == pl.num_programs(1) - 1)
    def _():
        sout_ref[...] = s_scr[...]


def _retention(pd, z, s0, layer, wts, cos_t, sin_t):
    bn, ln, _ = pd.shape
    lc = CHUNK if ln % CHUNK == 0 else ln
    lt = _pick_tile(ln, (256,))
    bb_n = 1 if ln > lt or bn < SUBLANES else SUBLANES
    hw = D_HEADS * D_QK
    tok = lambda j: pl.BlockSpec((bb_n, lt, hw), lambda b, t: (b, t, j))
    tab = pl.BlockSpec((lt, LANES), lambda b, t: (t, 0))
    dec = wts["d_tables"][lc]
    return pl.pallas_call(
        functools.partial(_ret_kernel, bb_n=bb_n, lt=lt, lc=lc),
        grid=(bn // bb_n, ln // lt),
        in_specs=[tok(0), tok(1), tok(2), tok(3), tab, tab,
                  _full((D_HEADS, lc, lc)), _full((D_HEADS, lc, LANES)),
                  _full((D_HEADS, lc, LANES)), _full((D_HEADS, 1, LANES)),
                  pl.BlockSpec((1, bb_n, D_HEADS, D_QK, D_V), lambda b, t: (layer, b, 0, 0, 0)),
                  _full((1, BRANCH_W)), _full((1, BRANCH_W))],
        out_specs=[tok(0),
                   pl.BlockSpec((bb_n, D_HEADS, D_QK, D_V), lambda b, t: (b, 0, 0, 0))],
        out_shape=[jax.ShapeDtypeStruct((bn, ln, BRANCH_W), BF16),
                   jax.ShapeDtypeStruct((bn, D_HEADS, D_QK, D_V), F32)],
        scratch_shapes=[pltpu.VMEM((bb_n, D_HEADS, D_QK, D_V), F32)],
        compiler_params=_params("parallel", "arbitrary"),
        name="retention",
    )(pd, pd, pd, z, cos_t, sin_t, dec["decay_mat"], dec["q_dec"], dec["k_dec"], dec["c_dec"],
      s0, wts["d_ln_g"], wts["d_ln_b"])


def _rope_tables(pos, d):
    inv = ROPE_THETA ** (-jnp.arange(0, d, 2, dtype=F32) / d)
    ang = pos.astype(F32)[:, None] * inv[None, :]
    cos = jnp.cos(ang)
    sin = jnp.sin(ang)
    reps = LANES // d
    return (jnp.tile(jnp.concatenate([cos, cos], -1), (1, reps)),
            jnp.tile(jnp.concatenate([-sin, sin], -1), (1, reps)))


def _retention_tables(lc):
    log_gamma = jnp.log1p(-jnp.exp2(-5.0 - jnp.arange(D_HEADS, dtype=F32)))
    idx = jnp.arange(lc, dtype=F32)
    rel = idx[:, None] - idx[None, :]
    decay_mat = jnp.where(rel >= 0, jnp.exp(jnp.maximum(rel, 0.0) * log_gamma[:, None, None]), 0.0)
    q_dec = jnp.exp((idx + 1.0)[None, :] * log_gamma[:, None])
    k_dec = jnp.exp((lc - 1.0 - idx)[None, :] * log_gamma[:, None])
    c_dec = jnp.exp(lc * log_gamma)
    bcast = lambda a: jnp.broadcast_to(a[..., None], a.shape + (LANES,))
    return {"decay_mat": decay_mat, "q_dec": bcast(q_dec), "k_dec": bcast(k_dec),
            "c_dec": bcast(c_dec[:, None])}


def _layer_weights(l, w_in, a_mu, a_w0, a_w_up, a_a0, a_a_up, a_k_k, a_k_a, a_r_k, a_ln_g, a_ln_b,
                   b_sinks, c_conv_w, c_conv_b, c_i_bias, c_f_bias, c_ln_g, c_ln_b,
                   d_ln_g, d_ln_b, w_branch, w_out, ln_g, ln_b, chunk_lens):
    o1 = A_COLS
    o2 = o1 + B_COLS
    o3 = o2 + C_COLS
    o4 = MIX_COLS
    o5 = o4 + N_BRANCH * BRANCH_W
    og = o2 + C_CONV_COLS + C_HEADS * C_V
    wl = w_in[l]
    seg = lambda a, b: wl[:, a:b].astype(BF16)
    row = lambda a: a[l].reshape(1, -1)
    zeros = jnp.zeros((A_DECAY_LORA, BRANCH_W), F32)
    lora = jnp.concatenate([jnp.concatenate([a_w_up[l], zeros], 1),
                            jnp.concatenate([zeros, a_a_up[l]], 1)], 0).astype(BF16)
    gate_b = jnp.pad(jnp.concatenate([c_i_bias[l], c_f_bias[l]]), (0, LANES - 2 * C_HEADS))
    return {
        "w_a": seg(0, o1), "w_b": seg(o1, o2), "w_cqk": seg(o2, o2 + C_CONV_COLS),
        "w_cv": seg(o2 + C_CONV_COLS, og),
        "w_cg": jnp.pad(wl[:, og:o3], ((0, 0), (0, LANES - 2 * C_HEADS))).astype(BF16),
        "w_d": seg(o3, o4), "w_z": seg(o4, o5), "w_g": seg(o5, wl.shape[1]),
        "a_mu": row(a_mu), "a_w0": row(a_w0), "a_a0": row(a_a0), "a_lora": lora,
        "a_k_k": row(a_k_k), "a_k_a": row(a_k_a), "a_r_k": row(a_r_k),
        "a_ln_g": row(a_ln_g), "a_ln_b": row(a_ln_b),
        "b_sinks": b_sinks[l],
        "c_conv_w": c_conv_w[l], "c_conv_b": row(c_conv_b), "c_gate_b": gate_b.reshape(1, LANES),
        "c_ln_g": row(c_ln_g), "c_ln_b": row(c_ln_b),
        "d_ln_g": row(d_ln_g), "d_ln_b": row(d_ln_b),
        "d_tables": {lc: _retention_tables(lc) for lc in chunk_lens},
        "w_branch": w_branch[l].astype(BF16), "w_out": w_out[l].astype(BF16),
        "ln_g": row(ln_g), "ln_b": row(ln_b),
    }


def _hybrid_layer(xf, xb, pos0, l, st, wts, tabs):
    bn, ln, _ = xf.shape
    m = bn * ln
    s_a, shift_a, k_buf, v_buf, c_m, n_m, m_m, conv_m, s_d = st
    tok = lambda a: a.reshape(bn, ln, a.shape[-1])
    pa = tok(_matmul(xb, wts["w_a"], name="proj_a"))
    pb = tok(_matmul(xb, wts["w_b"], name="proj_b"))
    pcqk = tok(_matmul(xb, wts["w_cqk"], name="proj_cqk"))
    pcv = tok(_matmul(xb, wts["w_cv"], name="proj_cv"))
    pcg = tok(_matmul(xb, wts["w_cg"], name="proj_cg"))
    pd = tok(_matmul(xb, wts["w_d"], name="proj_d"))
    z = tok(_matmul(xb, wts["w_z"], act="silu", name="proj_z"))
    gate = _matmul(xb, wts["w_g"], act="sigmoid", name="proj_gate")

    y_a, s_a_new = _rwkv(pa, z, shift_a, s_a, l, wts)
    shift_new = pa[:, -1]

    y_b, k_rot = _swa(pb, z, k_buf, v_buf, l, wts["b_sinks"], tabs["cos64"], tabs["sin64"], pos0)
    qw = B_HEADS * B_HD
    kvw = B_KV_HEADS * B_HD
    kv_shape = (bn, WINDOW, B_KV_HEADS, B_HD)
    k_new = jnp.concatenate([k_buf[l], k_rot], 1)[:, -WINDOW:].reshape(kv_shape)
    v_new = jnp.concatenate([v_buf[l], pb[..., qw + kvw:]], 1)[:, -WINDOW:].reshape(kv_shape)

    y_c, c_new, n_new, m_new = _mlstm(pcqk, pcv, pcg, z, conv_m, c_m, n_m, m_m, l, wts)
    conv_new = jnp.concatenate([conv_m[l, :, SUBLANES - (C_CONV - 1):], pcqk], 1)[:, -(C_CONV - 1):]

    y_d, s_d_new = _retention(pd, z, s_d, l, wts, tabs["cos128"], tabs["sin128"])

    flat = lambda a: a.reshape(m, a.shape[-1])
    merged = _merge([flat(y_a), flat(y_b), flat(y_c), flat(y_d)], gate, wts["w_branch"])
    xf_new, xb_new = _out_ln(merged, wts["w_out"], flat(xf), wts["ln_g"], wts["ln_b"])
    new = (s_a_new, shift_new, k_new, v_new, c_new, n_new[:, :, 0], m_new[:, :, 0, 0],
           conv_new, s_d_new)
    return xf_new.reshape(bn, ln, D_MODEL), xb_new, new


def _trunk(x, pos0, states, layer_wts):
    bn, ln, _ = x.shape
    s_a, shift_a, k_buf, v_buf, c_m, n_m, m_m, conv_m, s_d = states
    depth = s_a.shape[0]
    st = (s_a,
          shift_a[:, :, None, :],
          k_buf.reshape(depth, bn, WINDOW, LANES),
          v_buf.reshape(depth, bn, WINDOW, LANES),
          c_m,
          n_m[:, :, :, None, :],
          jnp.broadcast_to(m_m[..., None, None], m_m.shape + (1, LANES)),
          jnp.pad(conv_m, ((0, 0), (0, 0), (SUBLANES - (C_CONV - 1), 0), (0, 0))),
          s_d)
    pos = pos0 + jnp.arange(ln)
    cos64, sin64 = _rope_tables(pos, B_HD)
    cos128, sin128 = _rope_tables(pos, D_QK)
    tabs = {"cos64": cos64, "sin64": sin64, "cos128": cos128, "sin128": sin128}
    xb = x.reshape(bn * ln, D_MODEL).astype(BF16)
    new = []
    for l in range(depth):
        x, xb, st_new = _hybrid_layer(x, xb, pos0, l, st, layer_wts[l], tabs)
        new.append(st_new)
    stacked = tuple(jnp.stack([s[j] for s in new]) for j in range(len(states)))
    return x, stacked


def kernel(x_prompt, x_sample, state_rwkv_S, state_rwkv_shift, cache_swa_k, cache_swa_v,
           state_mlstm_C, state_mlstm_n, state_mlstm_m, state_mlstm_conv, state_ret_S,
           w_in, a_mu, a_w0, a_w_up, a_a0, a_a_up, a_k_k, a_k_a, a_r_k, a_ln_g, a_ln_b,
           b_sinks, c_conv_w, c_conv_b, c_i_bias, c_f_bias, c_ln_g, c_ln_b,
           d_ln_g, d_ln_b, w_branch, w_out, ln_g, ln_b):
    weights = (w_in, a_mu, a_w0, a_w_up, a_a0, a_a_up, a_k_k, a_k_a, a_r_k, a_ln_g, a_ln_b,
               b_sinks, c_conv_w, c_conv_b, c_i_bias, c_f_bias, c_ln_g, c_ln_b,
               d_ln_g, d_ln_b, w_branch, w_out, ln_g, ln_b)
    sample_states = (state_rwkv_S, state_rwkv_shift, cache_swa_k, cache_swa_v,
                     state_mlstm_C, state_mlstm_n, state_mlstm_m, state_mlstm_conv, state_ret_S)
    depth = w_in.shape[0]
    chunk_len = lambda ln: CHUNK if ln % CHUNK == 0 else ln
    chunk_lens = {chunk_len(x_prompt.shape[1]), chunk_len(x_sample.shape[1])}
    layer_wts = [_layer_weights(l, *weights, chunk_lens) for l in range(depth)]
    n_prompt = x_prompt.shape[0]
    zero_states = tuple(jnp.zeros((depth, n_prompt) + s.shape[2:], s.dtype) for s in sample_states)
    y_prompt, p = _trunk(x_prompt, 0, zero_states, layer_wts)
    y_sample, s = _trunk(x_sample, PAST_LEN, sample_states, layer_wts)
    return (y_prompt, y_sample, p[0], s[0], p[1], s[1], p[2], s[2], p[3], s[3], p[4], s[4],
            p[5], s[5], p[6], s[6], p[7], s[7], p[8], s[8])
```

```python
import functools

import jax
import jax.numpy as jnp
from jax import lax
from jax.experimental import pallas as pl
from jax.experimental.pallas import tpu as pltpu

F32 = jnp.float32
BF16 = jnp.bfloat16

D_MODEL = 2048
DEPTH = 2
PAST_LEN = 8192
N_BRANCH = 4
BRANCH_W = D_MODEL // 2
A_HD = 64
A_HEADS = BRANCH_W // A_HD
A_DECAY_LORA = 64
A_AAA_LORA = 64
B_HD = 64
B_HEADS = BRANCH_W // B_HD
B_KV_HEADS = B_HEADS // 8
WINDOW = 128
ROPE_THETA = 10000.0
C_HEADS = 8
C_V = BRANCH_W // C_HEADS
C_QK = C_V // 2
C_CONV = 4
D_HEADS = 8
D_QK = BRANCH_W // D_HEADS
D_V = BRANCH_W // D_HEADS
CHUNK = 64
LN_EPS = 1e-5
NEG_INF = -1e30
ALPHA = (2 * DEPTH) ** 0.25

A_COLS = 3 * BRANCH_W + A_DECAY_LORA + A_AAA_LORA
B_COLS = (B_HEADS + 2 * B_KV_HEADS) * B_HD
C_CONV_COLS = 2 * C_HEADS * C_QK
C_COLS = C_CONV_COLS + C_HEADS * C_V + 2 * C_HEADS
D_COLS = D_HEADS * (2 * D_QK + D_V)
MIX_COLS = A_COLS + B_COLS + C_COLS + D_COLS

LANES = 128
SUBLANES = 8
VMEM_LIMIT_BYTES = 56 * 1024 * 1024

NT_DIMS = (((1,), (1,)), ((), ()))
TN_DIMS = (((0,), (0,)), ((), ()))


def _params(*sem):
    return pltpu.CompilerParams(dimension_semantics=sem, vmem_limit_bytes=VMEM_LIMIT_BYTES)


def _full(shape):
    n = len(shape)
    return pl.BlockSpec(shape, lambda *_: (0,) * n)


def _softplus(x):
    return jnp.maximum(x, 0.0) + jnp.log1p(jnp.exp(-jnp.abs(x)))


def _split2(x):
    hi = x.astype(BF16)
    lo = (x - hi.astype(F32)).astype(BF16)
    return hi, lo


def _pair_ones():
    r = lax.broadcasted_iota(jnp.int32, (LANES, LANES), 0) // A_HD
    c = lax.broadcasted_iota(jnp.int32, (LANES, LANES), 1) // A_HD
    return jnp.where(r == c, 1.0, 0.0).astype(BF16)


def _seg_sum(x, bd):
    hi, lo = _split2(x)
    return (jnp.dot(hi, bd, preferred_element_type=F32)
            + jnp.dot(lo, bd, preferred_element_type=F32))


def _ln_lanes(x, g, b):
    mu = jnp.mean(x, axis=-1, keepdims=True)
    d = x - mu
    var = jnp.mean(d * d, axis=-1, keepdims=True)
    return d * lax.rsqrt(var + LN_EPS) * g + b


def _mm_kernel(x_ref, w_ref, o_ref, *, act):
    acc = jnp.dot(x_ref[...], w_ref[...], preferred_element_type=F32)
    if act == "silu":
        acc = acc * jax.nn.sigmoid(acc)
    elif act == "sigmoid":
        acc = jax.nn.sigmoid(acc)
    o_ref[...] = acc.astype(o_ref.dtype)


def _pick_tile(n, candidates):
    for c in candidates:
        if n % c == 0:
            return c
    return n


def _matmul(x, w, act=None, name="proj"):
    m, k = x.shape
    n = w.shape[1]
    tm = _pick_tile(m, (1024,))
    tn = _pick_tile(n, (1024, 640, 512, 256, 128))
    return pl.pallas_call(
        functools.partial(_mm_kernel, act=act),
        grid=(m // tm, n // tn),
        in_specs=[pl.BlockSpec((tm, k), lambda i, j: (i, 0)),
                  pl.BlockSpec((k, tn), lambda i, j: (0, j))],
        out_specs=pl.BlockSpec((tm, tn), lambda i, j: (i, j)),
        out_shape=jax.ShapeDtypeStruct((m, n), F32),
        compiler_params=_params("parallel", "arbitrary"),
        name=name,
    )(x, w)


def _merge_kernel(ya_ref, yb_ref, yc_ref, yd_ref, g_ref, w_ref, o_ref, acc_ref):
    i = pl.program_id(1)

    @pl.when(i == 0)
    def _():
        acc_ref[...] = jnp.zeros_like(acc_ref)

    for b, y_ref in enumerate((ya_ref, yb_ref, yc_ref, yd_ref)):
        @pl.when(i == b)
        def _(y_ref=y_ref):
            acc_ref[...] += g_ref[...] * jnp.dot(y_ref[...], w_ref[0],
                                                 preferred_element_type=F32)

    @pl.when(i == N_BRANCH - 1)
    def _():
        o_ref[...] = acc_ref[...].astype(o_ref.dtype)


def _merge(ys, gate, w_branch):
    m = gate.shape[0]
    tm = _pick_tile(m, (512,))
    y_spec = pl.BlockSpec((tm, BRANCH_W), lambda i, j: (i, 0))
    return pl.pallas_call(
        _merge_kernel,
        grid=(m // tm, N_BRANCH),
        in_specs=[y_spec, y_spec, y_spec, y_spec,
                  pl.BlockSpec((tm, D_MODEL), lambda i, j: (i, j)),
                  pl.BlockSpec((1, BRANCH_W, D_MODEL), lambda i, j: (j, 0, 0))],
        out_specs=pl.BlockSpec((tm, D_MODEL), lambda i, j: (i, 0)),
        out_shape=jax.ShapeDtypeStruct((m, D_MODEL), BF16),
        scratch_shapes=[pltpu.VMEM((tm, D_MODEL), F32)],
        compiler_params=_params("parallel", "arbitrary"),
        name="merge",
    )(*ys, gate, w_branch)


def _outln_kernel(m_ref, w_ref, x_ref, g_ref, b_ref, of_ref, ob_ref):
    out = jnp.dot(m_ref[...], w_ref[...], preferred_element_type=F32)
    y = _ln_lanes(ALPHA * x_ref[...] + out, g_ref[...], b_ref[...])
    of_ref[...] = y
    ob_ref[...] = y.astype(BF16)


def _out_ln(merged, w_out, x, g, b):
    m = x.shape[0]
    tm = _pick_tile(m, (512,))
    row = pl.BlockSpec((tm, D_MODEL), lambda i: (i, 0))
    return pl.pallas_call(
        _outln_kernel,
        grid=(m // tm,),
        in_specs=[row, _full((D_MODEL, D_MODEL)), row, _full((1, D_MODEL)), _full((1, D_MODEL))],
        out_specs=[row, row],
        out_shape=[jax.ShapeDtypeStruct((m, D_MODEL), F32),
                   jax.ShapeDtypeStruct((m, D_MODEL), BF16)],
        compiler_params=_params("parallel"),
        name="out_ln",
    )(merged, w_out, x, g, b)


N_PAIR = A_HEADS // 2
N_QUAD = N_PAIR // 2
CHAINS_PER_DOT = 4


def _quad_ones():
    n = 2 * LANES
    r = lax.broadcasted_iota(jnp.int32, (n, n), 0) // A_HD
    c = lax.broadcasted_iota(jnp.int32, (n, n), 1) // A_HD
    return jnp.where(r == c, 1.0, 0.0).astype(BF16)


def _rwkv_kernel(pa_ref, sh_ref, s0_ref, z_ref, mu_ref, w0_ref, a0_ref, wl_ref, kk_ref, ka_ref,
                 rk_ref, g_ref, b_ref, y_ref, sout_ref,
                 s_scr, sb_scr, prev_scr, r_s, w_s, k_s, v_s, kap_s, beta_s, y_s, *, bb_n, lt):
    t = pl.program_id(1)
    bd = _pair_ones()
    bd2 = _quad_ones()

    @pl.when(t == 0)
    def _():
        prev_scr[...] = sh_ref[0]
        for bb in range(bb_n):
            for p in range(N_PAIR):
                s = jnp.concatenate([s0_ref[0, bb, 2 * p], s0_ref[0, bb, 2 * p + 1]], axis=1)
                s_scr[bb, p] = s
                sb_scr[bb, p] = s.astype(BF16)

    w3 = 3 * BRANCH_W
    for bb in range(bb_n):
        pa = pa_ref[bb]
        row = lax.broadcasted_iota(jnp.int32, pa.shape, 0)
        prev = jnp.where(row == 0, prev_scr[bb], pltpu.roll(pa, 1, axis=0))
        prev_scr[bb] = pa[lt - 1:lt, :]
        xs = pa + (prev - pa) * mu_ref[...]
        lo = xs[:, w3:]
        lane = lax.broadcasted_iota(jnp.int32, lo.shape, 1)
        lo = jnp.where(lane < A_DECAY_LORA, jnp.tanh(lo), lo)
        pre = jnp.dot(lo.astype(BF16), wl_ref[...], preferred_element_type=F32)
        for p in range(N_PAIR):
            sl = slice(p * LANES, (p + 1) * LANES)
            r = xs[:, sl]
            k = xs[:, BRANCH_W + p * LANES:BRANCH_W + (p + 1) * LANES]
            v = xs[:, 2 * BRANCH_W + p * LANES:2 * BRANCH_W + (p + 1) * LANES]
            w_log = -_softplus(-(w0_ref[:, sl] + pre[:, sl])) - 0.5
            a = jax.nn.sigmoid(a0_ref[:, sl]
                               + pre[:, BRANCH_W + p * LANES:BRANCH_W + (p + 1) * LANES])
            kk = k * kk_ref[:, sl]
            kk = kk / jnp.maximum(jnp.sqrt(_seg_sum(kk * kk, bd)), 1e-12)
            r_s[bb, :, sl] = r
            w_s[bb, :, sl] = jnp.exp(-jnp.exp(w_log))
            k_s[bb, :, sl] = k * (1.0 + (a - 1.0) * ka_ref[:, sl])
            v_s[bb, :, sl] = v
            kap_s[bb, :, sl] = kk
            beta_s[bb, :, sl] = kk * a

    vi = lax.broadcasted_iota(jnp.int32, (A_HD, LANES), 0)
    li = lax.broadcasted_iota(jnp.int32, (A_HD, LANES), 1)
    diag = jnp.where((li % A_HD) == vi, 1.0, 0.0)
    diag_bf = diag.astype(BF16)
    row8 = lax.broadcasted_iota(jnp.int32, (SUBLANES, LANES), 0)
    chains = [(bb, q) for bb in range(bb_n) for q in range(N_QUAD)]
    groups = [chains[i:i + CHAINS_PER_DOT] for i in range(0, len(chains), CHAINS_PER_DOT)]
    lanes_of = lambda p: slice(p * LANES, (p + 1) * LANES)
    two = 2 * A_HD

    def steps(c, carry):
        rows = pl.ds(pl.multiple_of(c * SUBLANES, SUBLANES), SUBLANES)
        ytile = {(bb, p): jnp.zeros((SUBLANES, LANES), F32)
                 for bb in range(bb_n) for p in range(N_PAIR)}
        for j in range(SUBLANES):
            rowj = lambda ref, bb, p: ref[bb, rows, lanes_of(p)][j:j + 1]
            rowj_bf = lambda ref, bb, p: jnp.broadcast_to(
                rowj(ref, bb, p), (A_HD, LANES)).astype(BF16)
            res1 = []
            for grp in groups:
                lhs = []
                for bb, q in grp:
                    ps, vds = [], []
                    for p in (2 * q, 2 * q + 1):
                        ps.append(sb_scr[bb, p] * rowj_bf(kap_s, bb, p))
                        vds.append(diag_bf * rowj_bf(v_s, bb, p))
                    lhs += [jnp.concatenate(ps, axis=1), jnp.concatenate(vds, axis=1)]
                res1.append(jnp.dot(jnp.concatenate(lhs, axis=0), bd2,
                                    preferred_element_type=F32))
            res2 = []
            for grp, res in zip(groups, res1):
                lhs = []
                for c_i, (bb, q) in enumerate(grp):
                    qs = []
                    for i, p in enumerate((2 * q, 2 * q + 1)):
                        u = res[c_i * two:c_i * two + A_HD, lanes_of(i)]
                        vb = res[c_i * two + A_HD:(c_i + 1) * two, lanes_of(i)]
                        s = (s_scr[bb, p] * rowj(w_s, bb, p) - u * rowj(beta_s, bb, p)
                             + vb * rowj(k_s, bb, p))
                        s_scr[bb, p] = s
                        sb = s.astype(BF16)
                        sb_scr[bb, p] = sb
                        qs.append(sb * rowj_bf(r_s, bb, p))
                    lhs.append(jnp.concatenate(qs, axis=1))
                res2.append(jnp.dot(jnp.concatenate(lhs, axis=0), bd2,
                                    preferred_element_type=F32))
            for grp, res in zip(groups, res2):
                for c_i, (bb, q) in enumerate(grp):
                    for i, p in enumerate((2 * q, 2 * q + 1)):
                        yb = res[c_i * A_HD:(c_i + 1) * A_HD, lanes_of(i)]
                        y_row = jnp.sum(yb * diag, axis=0, keepdims=True)
                        ytile[bb, p] = jnp.where(row8 == j, y_row, ytile[bb, p])
        for (bb, p), tile in ytile.items():
            y_s[bb, rows, lanes_of(p)] = tile
        return carry

    lax.fori_loop(0, lt // SUBLANES, steps, 0)

    inv_hd = 1.0 / A_HD
    for bb in range(bb_n):
        for p in range(N_PAIR):
            sl = lanes_of(p)
            y = y_s[bb, :, sl]
            d = y - _seg_sum(y, bd) * inv_hd
            var = _seg_sum(d * d, bd) * inv_hd
            yn = d * lax.rsqrt(var + LN_EPS) * g_ref[:, sl] + b_ref[:, sl]
            bonus = _seg_sum(r_s[bb, :, sl] * k_s[bb, :, sl] * rk_ref[:, sl], bd) * v_s[bb, :, sl]
            y_ref[bb, :, sl] = ((yn + bonus) * z_ref[bb, :, sl]).astype(BF16)

    @pl.when(t == pl.num_programs(1) - 1)
    def _():
        for bb in range(bb_n):
            for p in range(N_PAIR):
                s = s_scr[bb, p]
                sout_ref[bb, 2 * p] = s[:, :A_HD]
                sout_ref[bb, 2 * p + 1] = s[:, A_HD:]


def _rwkv(pa, z, shift, s0, layer, wts):
    bn, ln, _ = pa.shape
    lt = _pick_tile(ln, (256,))
    bb_n = _pick_tile(bn, (2,) if ln > lt else (4, 2))
    row = lambda n: _full((1, n))
    tok = lambda n, j: pl.BlockSpec((bb_n, lt, n), lambda b, t: (b, t, j))
    scr = lambda: pltpu.VMEM((bb_n, lt, BRANCH_W), F32)
    return pl.pallas_call(
        functools.partial(_rwkv_kernel, bb_n=bb_n, lt=lt),
        grid=(bn // bb_n, ln // lt),
        in_specs=[tok(A_COLS, 0),
                  pl.BlockSpec((1, bb_n, 1, A_COLS), lambda b, t: (layer, b, 0, 0)),
                  pl.BlockSpec((1, bb_n, A_HEADS, A_HD, A_HD), lambda b, t: (layer, b, 0, 0, 0)),
                  tok(BRANCH_W, 0),
                  row(A_COLS), row(BRANCH_W), row(BRANCH_W), _full((LANES, 2 * BRANCH_W)),
                  row(BRANCH_W), row(BRANCH_W), row(BRANCH_W), row(BRANCH_W), row(BRANCH_W)],
        out_specs=[tok(BRANCH_W, 0),
                   pl.BlockSpec((bb_n, A_HEADS, A_HD, A_HD), lambda b, t: (b, 0, 0, 0))],
        out_shape=[jax.ShapeDtypeStruct((bn, ln, BRANCH_W), BF16),
                   jax.ShapeDtypeStruct((bn, A_HEADS, A_HD, A_HD), F32)],
        scratch_shapes=[pltpu.VMEM((bb_n, N_PAIR, A_HD, LANES), F32),
                        pltpu.VMEM((bb_n, N_PAIR, A_HD, LANES), BF16),
                        pltpu.VMEM((bb_n, 1, A_COLS), F32),
                        scr(), scr(), scr(), scr(), scr(), scr(), scr()],
        compiler_params=_params("parallel", "arbitrary"),
        name="rwkv7",
    )(pa, shift, s0, z, wts["a_mu"], wts["a_w0"], wts["a_a0"], wts["a_lora"], wts["a_k_k"],
      wts["a_k_a"], wts["a_r_k"], wts["a_ln_g"], wts["a_ln_b"])


def _rope_swap(x, half):
    w = x.shape[-1]
    if 2 * half == w:
        return pltpu.roll(x, half, axis=1)
    lane = lax.broadcasted_iota(jnp.int32, x.shape, 1)
    return jnp.where(lane % (2 * half) < half,
                     pltpu.roll(x, w - half, axis=1), pltpu.roll(x, half, axis=1))


def _swa_kernel(sink_ref, q_ref, k_ref, v_ref, z_ref, cs_ref, sn_ref, kb_ref, vb_ref,
                y_ref, ko_ref, pk_scr, pv_scr, *, bb_n, lb, nb, hs, pos0):
    n = pl.program_id(1)

    @pl.when(n == 0)
    def _():
        pk_scr[...] = kb_ref[0]
        pv_scr[...] = vb_ref[0]

    cs = cs_ref[...]
    sn = sn_ref[...]
    cs_q = jnp.concatenate([cs] * (B_HEADS * B_HD // LANES), axis=1)
    sn_q = jnp.concatenate([sn] * (B_HEADS * B_HD // LANES), axis=1)
    rows = hs * lb
    r_idx = lax.broadcasted_iota(jnp.int32, (rows, WINDOW + lb), 0)
    a_idx = lax.rem(r_idx, lb)
    c_idx = lax.broadcasted_iota(jnp.int32, (rows, WINDOW + lb), 1)
    key_pos = pos0 + n * lb - WINDOW + c_idx
    mask = (c_idx >= a_idx) & (c_idx <= WINDOW + a_idx) & (key_pos >= 0)
    blk = lax.broadcasted_iota(jnp.int32, (rows, 1), 0) // lb
    sinks = []
    for h0 in range(0, B_HEADS, hs):
        col = jnp.full((rows, 1), sink_ref[h0], F32)
        for i in range(1, hs):
            col = jnp.where(blk == i, sink_ref[h0 + i], col)
        sinks.append(col)
    group = B_HEADS // B_KV_HEADS
    ones_kv = jnp.ones((WINDOW + lb, B_HD), BF16)

    def per_b(bb, carry):
        q = q_ref[bb]
        q = q * cs_q + _rope_swap(q, B_HD // 2) * sn_q
        k = k_ref[bb]
        k = k * cs + _rope_swap(k, B_HD // 2) * sn
        v = v_ref[bb]
        ko_ref[bb] = k
        keys = jnp.concatenate([pk_scr[bb], k], axis=0).astype(BF16)
        vals = jnp.concatenate([pv_scr[bb], v], axis=0).astype(BF16)
        if nb > 1:
            pk_scr[bb] = k
            pv_scr[bb] = v
        scores = []
        for h0 in range(0, B_HEADS, hs):
            g = h0 // group
            qs = jnp.concatenate([q[:, h * B_HD:(h + 1) * B_HD] for h in range(h0, h0 + hs)],
                                 axis=0).astype(BF16)
            scores.append(lax.dot_general(qs, keys[:, g * B_HD:(g + 1) * B_HD], NT_DIMS,
                                          preferred_element_type=F32))
        probs, tails = [], []
        for s, sink in zip(scores, sinks):
            s = jnp.where(mask, s * (B_HD ** -0.5), NEG_INF)
            m = jnp.maximum(jnp.max(s, axis=-1, keepdims=True), sink)
            probs.append(jnp.exp(s - m).astype(BF16))
            tails.append(jnp.exp(sink - m))
        outs = []
        for i, (p, tail) in enumerate(zip(probs, tails)):
            g = (i * hs) // group
            pv = jnp.dot(p, vals[:, g * B_HD:(g + 1) * B_HD], preferred_element_type=F32)
            psum = jnp.dot(p, ones_kv, preferred_element_type=F32)
            o = pv / (psum + tail)
            outs.extend(o[j * lb:(j + 1) * lb] for j in range(hs))
        for j in range(B_HEADS // 2):
            sl = slice(j * LANES, (j + 1) * LANES)
            o = jnp.concatenate([outs[2 * j], outs[2 * j + 1]], axis=1)
            y_ref[bb, :, sl] = (o * z_ref[bb, :, sl]).astype(BF16)
        return carry

    lax.fori_loop(0, bb_n, per_b, 0)


def _swa(pb, z, k_buf, v_buf, layer, sinks, cos_t, sin_t, pos0):
    bn, ln, _ = pb.shape
    lb = WINDOW if ln % WINDOW == 0 else ln
    nb = ln // lb
    bb_n = 1 if nb > 1 else _pick_tile(bn, (8,))
    group = B_HEADS // B_KV_HEADS
    hs = group if group * lb <= LANES else 1
    qw = B_HEADS * B_HD
    tok = lambda n, j: pl.BlockSpec((bb_n, lb, n), lambda b, t: (b, t, j))
    buf = pl.BlockSpec((1, bb_n, WINDOW, LANES), lambda b, t: (layer, b, 0, 0))
    tab = pl.BlockSpec((lb, LANES), lambda b, t: (t, 0))
    return pl.pallas_call(
        functools.partial(_swa_kernel, bb_n=bb_n, lb=lb, nb=nb, hs=hs, pos0=pos0),
        grid=(bn // bb_n, nb),
        in_specs=[pl.BlockSpec(memory_space=pltpu.SMEM),
                  tok(qw, 0), tok(LANES, qw // LANES), tok(LANES, qw // LANES + 1),
                  tok(BRANCH_W, 1), tab, tab, buf, buf],
        out_specs=[tok(BRANCH_W, 0), tok(LANES, 0)],
        out_shape=[jax.ShapeDtypeStruct((bn, ln, BRANCH_W), BF16),
                   jax.ShapeDtypeStruct((bn, ln, LANES), F32)],
        scratch_shapes=[pltpu.VMEM((bb_n, WINDOW, LANES), F32),
                        pltpu.VMEM((bb_n, WINDOW, LANES), F32)],
        compiler_params=_params("parallel", "arbitrary"),
        name="swa",
    )(sinks, pb, pb, pb, z, cos_t, sin_t, k_buf, v_buf)


def _split3(x):
    hi = x.astype(BF16)
    rem = x - hi.astype(F32)
    mid = rem.astype(BF16)
    return hi, mid, (rem - mid.astype(F32)).astype(BF16)


def _dot_sel(x, sel, dims=(((1,), (0,)), ((), ()))):
    return sum(lax.dot_general(p, sel, dims, preferred_element_type=F32) for p in _split3(x))


def _cols_to_rows(x, n_rows):
    r = lax.broadcasted_iota(jnp.int32, (n_rows, LANES), 0)
    c = lax.broadcasted_iota(jnp.int32, (n_rows, LANES), 1)
    sel = jnp.where(r == c, 1.0, 0.0).astype(BF16)
    return sum(lax.dot_general(sel, p, NT_DIMS, preferred_element_type=F32) for p in _split3(x))


def _mlstm_kernel(qk_ref, v_ref, gt_ref, z_ref, cb_ref, cw_ref, cbias_ref, gb_ref,
                  c0_ref, n0_ref, n0h_ref, m0_ref, g_ref, b_ref, eli_ref, eb_ref,
                  y_ref, cout_ref, nout_ref, mout_ref,
                  c_scr, n_scr, nb_scr, m_scr, prev_scr, act_scr, gl_scr, *, bb_n, lt, lc):
    t = pl.program_id(1)

    @pl.when(t == 0)
    def _():
        c_scr[...] = c0_ref[0]
        n_scr[...] = n0_ref[0]
        m_scr[...] = m0_ref[0]
        prev_scr[...] = cb_ref[0]
        for bb in range(bb_n):
            nb = _dot_sel(n0h_ref[0, bb], eli_ref[:C_HEADS, :], TN_DIMS)
            for h in range(C_HEADS):
                nb_scr[bb, h] = nb[:, h * LANES:(h + 1) * LANES]

    row8 = lax.broadcasted_iota(jnp.int32, (SUBLANES, C_CONV_COLS), 0)
    lane = lax.broadcasted_iota(jnp.int32, (lt, LANES), 1)
    qi = lax.broadcasted_iota(jnp.int32, (lc, lc), 0)
    ki = lax.broadcasted_iota(jnp.int32, (lc, lc), 1)
    tril = ki <= qi
    tril_bf = jnp.where(tril, 1.0, 0.0).astype(BF16)
    ones_bf = jnp.ones((lc, LANES), BF16)
    qkw = C_HEADS * C_QK

    def per_b(bb, carry):
        u = qk_ref[bb]
        prev8 = prev_scr[bb]
        conv = cbias_ref[...] + u * cw_ref[C_CONV - 1:C_CONV, :]
        for s in range(1, C_CONV):
            sh = pltpu.roll(u, s, axis=0)
            head = jnp.where(row8 < s, pltpu.roll(prev8, s, axis=0), sh[:SUBLANES])
            sh = head if lt == SUBLANES else jnp.concatenate([head, sh[SUBLANES:]], axis=0)
            conv = conv + sh * cw_ref[C_CONV - 1 - s:C_CONV - s, :]
        act_scr[...] = conv * jax.nn.sigmoid(conv)
        prev_scr[bb] = u[lt - SUBLANES:, :]
        g = gt_ref[bb] + gb_ref[...]
        gl_scr[...] = jnp.where(lane < C_HEADS, g, -_softplus(-g))

        def chunk(r0):
            glc = gl_scr[pl.ds(r0, lc), :]
            cum = sum(jnp.dot(tril_bf, p, preferred_element_type=F32) for p in _split3(glc))
            glr = _cols_to_rows(glc, C_HEADS)
            cur = _cols_to_rows(cum, 2 * C_HEADS)
            li_x = _dot_sel(glc, eli_ref[...])
            b_x = _dot_sel(cum, eb_ref[...])
            actc = act_scr[pl.ds(r0, lc), :]
            vc = v_ref[bb, pl.ds(r0, lc), :]
            zc = z_ref[bb, pl.ds(r0, lc), :]
            heads = range(C_HEADS)
            vsl = lambda h: slice(h * C_V, (h + 1) * C_V)
            qbs, vs, qk, qc, qn, kws, gate = [], [], [], [], [], [], []
            for h in heads:
                q = actc[:, h * C_QK:(h + 1) * C_QK]
                k = actc[:, qkw + h * C_QK:qkw + (h + 1) * C_QK] * (C_QK ** -0.5)
                qb = q.astype(BF16)
                vs.append(vc[:, vsl(h)].astype(BF16))
                qk.append(lax.dot_general(qb, k.astype(BF16), NT_DIMS,
                                          preferred_element_type=F32))
                qc.append(jnp.dot(qb, c_scr[bb, h].astype(BF16), preferred_element_type=F32))
                qn.append(jnp.dot(qb, nb_scr[bb, h].astype(BF16), preferred_element_type=F32))
                b_q = b_x[:, vsl(h)]
                li_row = glr[h:h + 1, :]
                b_row = cur[C_HEADS + h:C_HEADS + h + 1, :]
                m_s = m_scr[bb, h]
                dmat = jnp.where(tril, b_q[:, :lc] - b_row + li_row, NEG_INF)
                inter = b_q + m_s
                mt = jnp.maximum(inter, jnp.max(dmat, axis=1, keepdims=True))
                b_last = b_row[:, lc - 1:lc]
                gl_row = b_last - b_row + li_row
                gl_col = b_last - b_q[:, :C_QK] + li_x[:, h * LANES:h * LANES + C_QK]
                m_new = jnp.maximum(b_last + m_s[:, :1], jnp.max(gl_row, axis=1, keepdims=True))
                kws.append(k * jnp.exp(gl_col - m_new))
                gate.append((jnp.exp(dmat - mt[:, :lc]), jnp.exp(inter - mt), mt, m_new,
                             jnp.exp(b_last + m_s[:, :1] - m_new)))
            nums, dens = [], []
            for h in heads:
                wts, sc, _, _, _ = gate[h]
                a = (qk[h] * wts).astype(BF16)
                nums.append(jnp.dot(a, vs[h], preferred_element_type=F32) + sc * qc[h])
                dens.append(jnp.dot(a, ones_bf, preferred_element_type=F32) + sc * qn[h])
            for h in heads:
                _, _, _, m_new, s_old = gate[h]
                upd = lax.dot_general(kws[h].astype(BF16),
                                      jnp.concatenate([vs[h], ones_bf], axis=1), TN_DIMS,
                                      preferred_element_type=F32)
                c_scr[bb, h] = s_old * c_scr[bb, h] + upd[:, :C_V]
                nb_scr[bb, h] = s_old * nb_scr[bb, h] + upd[:, C_V:]
                n_scr[bb, h] = s_old * n_scr[bb, h] + jnp.sum(kws[h], axis=0, keepdims=True)
                m_scr[bb, h] = jnp.broadcast_to(m_new, (1, LANES))
            for h in heads:
                hh = nums[h] / jnp.maximum(jnp.abs(dens[h]), jnp.exp(-gate[h][2]))
                yn = _ln_lanes(hh, g_ref[:, vsl(h)], b_ref[:, vsl(h)])
                y_ref[bb, pl.ds(r0, lc), vsl(h)] = (yn * zc[:, vsl(h)]).astype(BF16)

        if lt == lc:
            chunk(0)
        else:
            def body(c, cc):
                chunk(pl.multiple_of(c * lc, lc))
                return cc
            lax.fori_loop(0, lt // lc, body, 0)
        return carry

    lax.fori_loop(0, bb_n, per_b, 0)

    @pl.when(t == pl.num_programs(1) - 1)
    def _():
        cout_ref[...] = c_scr[...]
        nout_ref[...] = n_scr[...]
        mout_ref[...] = m_scr[...]


def _mlstm(qk, v, gates, z, conv8, c0, n0, m0, layer, wts):
    bn, ln, _ = qk.shape
    lc = CHUNK if ln % CHUNK == 0 else ln
    lt = _pick_tile(ln, (256,))
    bb_n = 1 if ln > lt or bn < SUBLANES else SUBLANES
    tok = lambda n, j: pl.BlockSpec((bb_n, lt, n), lambda b, t: (b, t, j))
    st_in = lambda *s: pl.BlockSpec((1, bb_n) + s, lambda b, t: (layer, b) + (0,) * len(s))
    st_out = lambda *s: pl.BlockSpec((bb_n,) + s, lambda b, t: (b,) + (0,) * len(s))
    row = lambda n: _full((1, n))
    col = lax.broadcasted_iota(jnp.int32, (LANES, C_HEADS * LANES), 0)
    blk = lax.broadcasted_iota(jnp.int32, (LANES, C_HEADS * LANES), 1) // LANES
    e_li = (col == blk).astype(BF16)
    e_b = (col == blk + C_HEADS).astype(BF16)
    return pl.pallas_call(
        functools.partial(_mlstm_kernel, bb_n=bb_n, lt=lt, lc=lc),
        grid=(bn // bb_n, ln // lt),
        in_specs=[tok(C_CONV_COLS, 0), tok(BRANCH_W, 0), tok(LANES, 0), tok(BRANCH_W, 2),
                  st_in(SUBLANES, C_CONV_COLS), _full((C_CONV, C_CONV_COLS)), row(C_CONV_COLS),
                  row(LANES),
                  st_in(C_HEADS, C_QK, C_V), st_in(C_HEADS, 1, C_QK), st_in(C_HEADS, C_QK),
                  st_in(C_HEADS, 1, LANES),
                  row(BRANCH_W), row(BRANCH_W),
                  _full((LANES, C_HEADS * LANES)), _full((LANES, C_HEADS * LANES))],
        out_specs=[tok(BRANCH_W, 0), st_out(C_HEADS, C_QK, C_V), st_out(C_HEADS, 1, C_QK),
                   st_out(C_HEADS, 1, LANES)],
        out_shape=[jax.ShapeDtypeStruct((bn, ln, BRANCH_W), BF16),
                   jax.ShapeDtypeStruct((bn, C_HEADS, C_QK, C_V), F32),
                   jax.ShapeDtypeStruct((bn, C_HEADS, 1, C_QK), F32),
                   jax.ShapeDtypeStruct((bn, C_HEADS, 1, LANES), F32)],
        scratch_shapes=[pltpu.VMEM((bb_n, C_HEADS, C_QK, C_V), F32),
                        pltpu.VMEM((bb_n, C_HEADS, 1, C_QK), F32),
                        pltpu.VMEM((bb_n, C_HEADS, C_QK, LANES), F32),
                        pltpu.VMEM((bb_n, C_HEADS, 1, LANES), F32),
                        pltpu.VMEM((bb_n, SUBLANES, C_CONV_COLS), F32),
                        pltpu.VMEM((lt, C_CONV_COLS), F32),
                        pltpu.VMEM((lt, LANES), F32)],
        compiler_params=_params("parallel", "arbitrary"),
        name="mlstm",
    )(qk, v, gates, z, conv8, wts["c_conv_w"], wts["c_conv_b"], wts["c_gate_b"],
      c0, n0, n0[:, :, :, 0, :], m0, wts["c_ln_g"], wts["c_ln_b"], e_li, e_b)


def _ret_kernel(q_ref, k_ref, v_ref, z_ref, cs_ref, sn_ref, dm_ref, qd_ref, kd_ref, cd_ref,
                s0_ref, g_ref, b_ref, y_ref, sout_ref, s_scr, *, bb_n, lt, lc):
    t = pl.program_id(1)

    @pl.when(t == 0)
    def _():
        s_scr[...] = s0_ref[0]

    heads = range(D_HEADS)
    lanes_of = lambda h: slice(h * LANES, (h + 1) * LANES)

    def per_b(bb, carry):
        def chunk(r0):
            rows = pl.ds(r0, lc)
            cs = cs_ref[rows, :]
            sn = sn_ref[rows, :]
            qs, ks, vs, inner, cross = [], [], [], [], []
            for h in heads:
                q = q_ref[bb, rows, lanes_of(h)]
                k = k_ref[bb, rows, lanes_of(h)]
                q = (q * cs + _rope_swap(q, D_QK // 2) * sn).astype(BF16)
                k = (k * cs + _rope_swap(k, D_QK // 2) * sn) * (D_QK ** -0.5)
                qs.append(q)
                ks.append(k)
                vs.append(v_ref[bb, rows, lanes_of(h)].astype(BF16))
                inner.append(lax.dot_general(q, k.astype(BF16), NT_DIMS,
                                             preferred_element_type=F32))
                cross.append(jnp.dot(q, s_scr[bb, h].astype(BF16), preferred_element_type=F32))
            outs = []
            for h in heads:
                a = (inner[h] * dm_ref[h]).astype(BF16)
                outs.append(jnp.dot(a, vs[h], preferred_element_type=F32) + cross[h] * qd_ref[h])
            for h in heads:
                s_scr[bb, h] = cd_ref[h] * s_scr[bb, h] + lax.dot_general(
                    (ks[h] * kd_ref[h]).astype(BF16), vs[h], TN_DIMS, preferred_element_type=F32)
            for h in heads:
                yn = _ln_lanes(outs[h], g_ref[:, lanes_of(h)], b_ref[:, lanes_of(h)])
                y_ref[bb, rows, lanes_of(h)] = (yn * z_ref[bb, rows, lanes_of(h)]).astype(BF16)

        if lt == lc:
            chunk(0)
        else:
            def body(c, cc):
                chunk(pl.multiple_of(c * lc, lc))
                return cc
            lax.fori_loop(0, lt // lc, body, 0)
        return carry

    lax.fori_loop(0, bb_n, per_b, 0)

    @pl.when(t == pl.num_programs(1) - 1)
    def _():
        sout_ref[...] = s_scr[...]


def _retention(pd, z, s0, layer, wts, cos_t, sin_t):
    bn, ln, _ = pd.shape
    lc = CHUNK if ln % CHUNK == 0 else ln
    lt = _pick_tile(ln, (256,))
    bb_n = 1 if ln > lt or bn < SUBLANES else SUBLANES
    hw = D_HEADS * D_QK
    tok = lambda j: pl.BlockSpec((bb_n, lt, hw), lambda b, t: (b, t, j))
    tab = pl.BlockSpec((lt, LANES), lambda b, t: (t, 0))
    dec = wts["d_tables"][lc]
    return pl.pallas_call(
        functools.partial(_ret_kernel, bb_n=bb_n, lt=lt, lc=lc),
        grid=(bn // bb_n, ln // lt),
        in_specs=[tok(0), tok(1), tok(2), tok(3), tab, tab,
                  _full((D_HEADS, lc, lc)), _full((D_HEADS, lc, LANES)),
                  _full((D_HEADS, lc, LANES)), _full((D_HEADS, 1, LANES)),
                  pl.BlockSpec((1, bb_n, D_HEADS, D_QK, D_V), lambda b, t: (layer, b, 0, 0, 0)),
                  _full((1, BRANCH_W)), _full((1, BRANCH_W))],
        out_specs=[tok(0),
                   pl.BlockSpec((bb_n, D_HEADS, D_QK, D_V), lambda b, t: (b, 0, 0, 0))],
        out_shape=[jax.ShapeDtypeStruct((bn, ln, BRANCH_W), BF16),
                   jax.ShapeDtypeStruct((bn, D_HEADS, D_QK, D_V), F32)],
        scratch_shapes=[pltpu.VMEM((bb_n, D_HEADS, D_QK, D_V), F32)],
        compiler_params=_params("parallel", "arbitrary"),
        name="retention",
    )(pd, pd, pd, z, cos_t, sin_t, dec["decay_mat"], dec["q_dec"], dec["k_dec"], dec["c_dec"],
      s0, wts["d_ln_g"], wts["d_ln_b"])


def _rope_tables(pos, d):
    inv = ROPE_THETA ** (-jnp.arange(0, d, 2, dtype=F32) / d)
    ang = pos.astype(F32)[:, None] * inv[None, :]
    cos = jnp.cos(ang)
    sin = jnp.sin(ang)
    reps = LANES // d
    return (jnp.tile(jnp.concatenate([cos, cos], -1), (1, reps)),
            jnp.tile(jnp.concatenate([-sin, sin], -1), (1, reps)))


def _retention_tables(lc):
    log_gamma = jnp.log1p(-jnp.exp2(-5.0 - jnp.arange(D_HEADS, dtype=F32)))
    idx = jnp.arange(lc, dtype=F32)
    rel = idx[:, None] - idx[None, :]
    decay_mat = jnp.where(rel >= 0, jnp.exp(jnp.maximum(rel, 0.0) * log_gamma[:, None, None]), 0.0)
    q_dec = jnp.exp((idx + 1.0)[None, :] * log_gamma[:, None])
    k_dec = jnp.exp((lc - 1.0 - idx)[None, :] * log_gamma[:, None])
    c_dec = jnp.exp(lc * log_gamma)
    bcast = lambda a: jnp.broadcast_to(a[..., None], a.shape + (LANES,))
    return {"decay_mat": decay_mat, "q_dec": bcast(q_dec), "k_dec": bcast(k_dec),
            "c_dec": bcast(c_dec[:, None])}


def _layer_weights(l, w_in, a_mu, a_w0, a_w_up, a_a0, a_a_up, a_k_k, a_k_a, a_r_k, a_ln_g, a_ln_b,
                   b_sinks, c_conv_w, c_conv_b, c_i_bias, c_f_bias, c_ln_g, c_ln_b,
                   d_ln_g, d_ln_b, w_branch, w_out, ln_g, ln_b, chunk_lens):
    o1 = A_COLS
    o2 = o1 + B_COLS
    o3 = o2 + C_COLS
    o4 = MIX_COLS
    o5 = o4 + N_BRANCH * BRANCH_W
    og = o2 + C_CONV_COLS + C_HEADS * C_V
    wl = w_in[l]
    seg = lambda a, b: wl[:, a:b].astype(BF16)
    row = lambda a: a[l].reshape(1, -1)
    zeros = jnp.zeros((A_DECAY_LORA, BRANCH_W), F32)
    lora = jnp.concatenate([jnp.concatenate([a_w_up[l], zeros], 1),
                            jnp.concatenate([zeros, a_a_up[l]], 1)], 0).astype(BF16)
    gate_b = jnp.pad(jnp.concatenate([c_i_bias[l], c_f_bias[l]]), (0, LANES - 2 * C_HEADS))
    return {
        "w_a": seg(0, o1), "w_b": seg(o1, o2), "w_cqk": seg(o2, o2 + C_CONV_COLS),
        "w_cv": seg(o2 + C_CONV_COLS, og),
        "w_cg": jnp.pad(wl[:, og:o3], ((0, 0), (0, LANES - 2 * C_HEADS))).astype(BF16),
        "w_d": seg(o3, o4), "w_z": seg(o4, o5), "w_g": seg(o5, wl.shape[1]),
        "a_mu": row(a_mu), "a_w0": row(a_w0), "a_a0": row(a_a0), "a_lora": lora,
        "a_k_k": row(a_k_k), "a_k_a": row(a_k_a), "a_r_k": row(a_r_k),
        "a_ln_g": row(a_ln_g), "a_ln_b": row(a_ln_b),
        "b_sinks": b_sinks[l],
        "c_conv_w": c_conv_w[l], "c_conv_b": row(c_conv_b), "c_gate_b": gate_b.reshape(1, LANES),
        "c_ln_g": row(c_ln_g), "c_ln_b": row(c_ln_b),
        "d_ln_g": row(d_ln_g), "d_ln_b": row(d_ln_b),
        "d_tables": {lc: _retention_tables(lc) for lc in chunk_lens},
        "w_branch": w_branch[l].astype(BF16), "w_out": w_out[l].astype(BF16),
        "ln_g": row(ln_g), "ln_b": row(ln_b),
    }


def _hybrid_layer(xf, xb, pos0, l, st, wts, tabs):
    bn, ln, _ = xf.shape
    m = bn * ln
    s_a, shift_a, k_buf, v_buf, c_m, n_m, m_m, conv_m, s_d = st
    tok = lambda a: a.reshape(bn, ln, a.shape[-1])
    pa = tok(_matmul(xb, wts["w_a"], name="proj_a"))
    pb = tok(_matmul(xb, wts["w_b"], name="proj_b"))
    pcqk = tok(_matmul(xb, wts["w_cqk"], name="proj_cqk"))
    pcv = tok(_matmul(xb, wts["w_cv"], name="proj_cv"))
    pcg = tok(_matmul(xb, wts["w_cg"], name="proj_cg"))
    pd = tok(_matmul(xb, wts["w_d"], name="proj_d"))
    z = tok(_matmul(xb, wts["w_z"], act="silu", name="proj_z"))
    gate = _matmul(xb, wts["w_g"], act="sigmoid", name="proj_gate")

    y_a, s_a_new = _rwkv(pa, z, shift_a, s_a, l, wts)
    shift_new = pa[:, -1]

    y_b, k_rot = _swa(pb, z, k_buf, v_buf, l, wts["b_sinks"], tabs["cos64"], tabs["sin64"], pos0)
    qw = B_HEADS * B_HD
    kvw = B_KV_HEADS * B_HD
    kv_shape = (bn, WINDOW, B_KV_HEADS, B_HD)
    k_new = jnp.concatenate([k_buf[l], k_rot], 1)[:, -WINDOW:].reshape(kv_shape)
    v_new = jnp.concatenate([v_buf[l], pb[..., qw + kvw:]], 1)[:, -WINDOW:].reshape(kv_shape)

    y_c, c_new, n_new, m_new = _mlstm(pcqk, pcv, pcg, z, conv_m, c_m, n_m, m_m, l, wts)
    conv_new = jnp.concatenate([conv_m[l, :, SUBLANES - (C_CONV - 1):], pcqk], 1)[:, -(C_CONV - 1):]

    y_d, s_d_new = _retention(pd, z, s_d, l, wts, tabs["cos128"], tabs["sin128"])

    flat = lambda a: a.reshape(m, a.shape[-1])
    merged = _merge([flat(y_a), flat(y_b), flat(y_c), flat(y_d)], gate, wts["w_branch"])
    xf_new, xb_new = _out_ln(merged, wts["w_out"], flat(xf), wts["ln_g"], wts["ln_b"])
    new = (s_a_new, shift_new, k_new, v_new, c_new, n_new[:, :, 0], m_new[:, :, 0, 0],
           conv_new, s_d_new)
    return xf_new.reshape(bn, ln, D_MODEL), xb_new, new


def _trunk(x, pos0, states, layer_wts):
    bn, ln, _ = x.shape
    s_a, shift_a, k_buf, v_buf, c_m, n_m, m_m, conv_m, s_d = states
    depth = s_a.shape[0]
    st = (s_a,
          shift_a[:, :, None, :],
          k_buf.reshape(depth, bn, WINDOW, LANES),
          v_buf.reshape(depth, bn, WINDOW, LANES),
          c_m,
          n_m[:, :, :, None, :],
          jnp.broadcast_to(m_m[..., None, None], m_m.shape + (1, LANES)),
          jnp.pad(conv_m, ((0, 0), (0, 0), (SUBLANES - (C_CONV - 1), 0), (0, 0))),
          s_d)
    pos = pos0 + jnp.arange(ln)
    cos64, sin64 = _rope_tables(pos, B_HD)
    cos128, sin128 = _rope_tables(pos, D_QK)
    tabs = {"cos64": cos64, "sin64": sin64, "cos128": cos128, "sin128": sin128}
    xb = x.reshape(bn * ln, D_MODEL).astype(BF16)
    new = []
    for l in range(depth):
        x, xb, st_new = _hybrid_layer(x, xb, pos0, l, st, layer_wts[l], tabs)
        new.append(st_new)
    stacked = tuple(jnp.stack([s[j] for s in new]) for j in range(len(states)))
    return x, stacked


def kernel(x_prompt, x_sample, state_rwkv_S, state_rwkv_shift, cache_swa_k, cache_swa_v,
           state_mlstm_C, state_mlstm_n, state_mlstm_m, state_mlstm_conv, state_ret_S,
           w_in, a_mu, a_w0, a_w_up, a_a0, a_a_up, a_k_k, a_k_a, a_r_k, a_ln_g, a_ln_b,
           b_sinks, c_conv_w, c_conv_b, c_i_bias, c_f_bias, c_ln_g, c_ln_b,
           d_ln_g, d_ln_b, w_branch, w_out, ln_g, ln_b):
    weights = (w_in, a_mu, a_w0, a_w_up, a_a0, a_a_up, a_k_k, a_k_a, a_r_k, a_ln_g, a_ln_b,
               b_sinks, c_conv_w, c_conv_b, c_i_bias, c_f_bias, c_ln_g, c_ln_b,
               d_ln_g, d_ln_b, w_branch, w_out, ln_g, ln_b)
    sample_states = (state_rwkv_S, state_rwkv_shift, cache_swa_k, cache_swa_v,
                     state_mlstm_C, state_mlstm_n, state_mlstm_m, state_mlstm_conv, state_ret_S)
    depth = w_in.shape[0]
    chunk_len = lambda ln: CHUNK if ln % CHUNK == 0 else ln
    chunk_lens = {chunk_len(x_prompt.shape[1]), chunk_len(x_sample.shape[1])}
    layer_wts = [_layer_weights(l, *weights, chunk_lens) for l in range(depth)]
    n_prompt = x_prompt.shape[0]
    zero_states = tuple(jnp.zeros((depth, n_prompt) + s.shape[2:], s.dtype) for s in sample_states)
    y_prompt, p = _trunk(x_prompt, 0, zero_states, layer_wts)
    y_sample, s = _trunk(x_sample, PAST_LEN, sample_states, layer_wts)
    return (y_prompt, y_sample, p[0], s[0], p[1], s[1], p[2], s[2], p[3], s[3], p[4], s[4],
            p[5], s[5], p[6], s[6], p[7], s[7], p[8], s[8])
```

```python
import functools

import jax
import jax.numpy as jnp
from jax import lax
from jax.experimental import pallas as pl
from jax.experimental.pallas import tpu as pltpu

F32 = jnp.float32
BF16 = jnp.bfloat16

D_MODEL = 2048
DEPTH = 2
PAST_LEN = 8192
N_BRANCH = 4
BRANCH_W = D_MODEL // 2
A_HD = 64
A_HEADS = BRANCH_W // A_HD
A_DECAY_LORA = 64
A_AAA_LORA = 64
B_HD = 64
B_HEADS = BRANCH_W // B_HD
B_KV_HEADS = B_HEADS // 8
WINDOW = 128
ROPE_THETA = 10000.0
C_HEADS = 8
C_V = BRANCH_W // C_HEADS
C_QK = C_V // 2
C_CONV = 4
D_HEADS = 8
D_QK = BRANCH_W // D_HEADS
D_V = BRANCH_W // D_HEADS
CHUNK = 64
LN_EPS = 1e-5
NEG_INF = -1e30
ALPHA = (2 * DEPTH) ** 0.25

A_COLS = 3 * BRANCH_W + A_DECAY_LORA + A_AAA_LORA
B_COLS = (B_HEADS + 2 * B_KV_HEADS) * B_HD
C_CONV_COLS = 2 * C_HEADS * C_QK
C_COLS = C_CONV_COLS + C_HEADS * C_V + 2 * C_HEADS
D_COLS = D_HEADS * (2 * D_QK + D_V)
MIX_COLS = A_COLS + B_COLS + C_COLS + D_COLS

LANES = 128
SUBLANES = 8
VMEM_LIMIT_BYTES = 56 * 1024 * 1024

NT_DIMS = (((1,), (1,)), ((), ()))
TN_DIMS = (((0,), (0,)), ((), ()))


def _params(*sem):
    return pltpu.CompilerParams(dimension_semantics=sem, vmem_limit_bytes=VMEM_LIMIT_BYTES)


def _full(shape):
    n = len(shape)
    return pl.BlockSpec(shape, lambda *_: (0,) * n)


def _softplus(x):
    return jnp.maximum(x, 0.0) + jnp.log1p(jnp.exp(-jnp.abs(x)))


def _split2(x):
    hi = x.astype(BF16)
    lo = (x - hi.astype(F32)).astype(BF16)
    return hi, lo


def _pair_ones():
    r = lax.broadcasted_iota(jnp.int32, (LANES, LANES), 0) // A_HD
    c = lax.broadcasted_iota(jnp.int32, (LANES, LANES), 1) // A_HD
    return jnp.where(r == c, 1.0, 0.0).astype(BF16)


def _seg_sum(x, bd):
    hi, lo = _split2(x)
    return (jnp.dot(hi, bd, preferred_element_type=F32)
            + jnp.dot(lo, bd, preferred_element_type=F32))


def _ln_lanes(x, g, b):
    mu = jnp.mean(x, axis=-1, keepdims=True)
    d = x - mu
    var = jnp.mean(d * d, axis=-1, keepdims=True)
    return d * lax.rsqrt(var + LN_EPS) * g + b


def _mm_kernel(x_ref, w_ref, o_ref, *, act):
    acc = jnp.dot(x_ref[...], w_ref[...], preferred_element_type=F32)
    if act == "silu":
        acc = acc * jax.nn.sigmoid(acc)
    elif act == "sigmoid":
        acc = jax.nn.sigmoid(acc)
    o_ref[...] = acc.astype(o_ref.dtype)


def _pick_tile(n, candidates):
    for c in candidates:
        if n % c == 0:
            return c
    return n


def _matmul(x, w, act=None, name="proj"):
    m, k = x.shape
    n = w.shape[1]
    tm = _pick_tile(m, (1024,))
    tn = _pick_tile(n, (1024, 640, 512, 256, 128))
    return pl.pallas_call(
        functools.partial(_mm_kernel, act=act),
        grid=(m // tm, n // tn),
        in_specs=[pl.BlockSpec((tm, k), lambda i, j: (i, 0)),
                  pl.BlockSpec((k, tn), lambda i, j: (0, j))],
        out_specs=pl.BlockSpec((tm, tn), lambda i, j: (i, j)),
        out_shape=jax.ShapeDtypeStruct((m, n), F32),
        compiler_params=_params("parallel", "arbitrary"),
        name=name,
    )(x, w)


def _merge_kernel(ya_ref, yb_ref, yc_ref, yd_ref, g_ref, w_ref, o_ref, acc_ref):
    i = pl.program_id(1)

    @pl.when(i == 0)
    def _():
        acc_ref[...] = jnp.zeros_like(acc_ref)

    for b, y_ref in enumerate((ya_ref, yb_ref, yc_ref, yd_ref)):
        @pl.when(i == b)
        def _(y_ref=y_ref):
            acc_ref[...] += g_ref[...] * jnp.dot(y_ref[...], w_ref[0],
                                                 preferred_element_type=F32)

    @pl.when(i == N_BRANCH - 1)
    def _():
        o_ref[...] = acc_ref[...].astype(o_ref.dtype)


def _merge(ys, gate, w_branch):
    m = gate.shape[0]
    tm = _pick_tile(m, (512,))
    y_spec = pl.BlockSpec((tm, BRANCH_W), lambda i, j: (i, 0))
    return pl.pallas_call(
        _merge_kernel,
        grid=(m // tm, N_BRANCH),
        in_specs=[y_spec, y_spec, y_spec, y_spec,
                  pl.BlockSpec((tm, D_MODEL), lambda i, j: (i, j)),
                  pl.BlockSpec((1, BRANCH_W, D_MODEL), lambda i, j: (j, 0, 0))],
        out_specs=pl.BlockSpec((tm, D_MODEL), lambda i, j: (i, 0)),
        out_shape=jax.ShapeDtypeStruct((m, D_MODEL), BF16),
        scratch_shapes=[pltpu.VMEM((tm, D_MODEL), F32)],
        compiler_params=_params("parallel", "arbitrary"),
        name="merge",
    )(*ys, gate, w_branch)


def _outln_kernel(m_ref, w_ref, x_ref, g_ref, b_ref, of_ref, ob_ref):
    out = jnp.dot(m_ref[...], w_ref[...], preferred_element_type=F32)
    y = _ln_lanes(ALPHA * x_ref[...] + out, g_ref[...], b_ref[...])
    of_ref[...] = y
    ob_ref[...] = y.astype(BF16)


def _out_ln(merged, w_out, x, g, b):
    m = x.shape[0]
    tm = _pick_tile(m, (512,))
    row = pl.BlockSpec((tm, D_MODEL), lambda i: (i, 0))
    return pl.pallas_call(
        _outln_kernel,
        grid=(m // tm,),
        in_specs=[row, _full((D_MODEL, D_MODEL)), row, _full((1, D_MODEL)), _full((1, D_MODEL))],
        out_specs=[row, row],
        out_shape=[jax.ShapeDtypeStruct((m, D_MODEL), F32),
                   jax.ShapeDtypeStruct((m, D_MODEL), BF16)],
        compiler_params=_params("parallel"),
        name="out_ln",
    )(merged, w_out, x, g, b)


N_PAIR = A_HEADS // 2
N_QUAD = N_PAIR // 2
CHAINS_PER_DOT = 8


def _quad_ones():
    n = 2 * LANES
    r = lax.broadcasted_iota(jnp.int32, (n, n), 0) // A_HD
    c = lax.broadcasted_iota(jnp.int32, (n, n), 1) // A_HD
    return jnp.where(r == c, 1.0, 0.0).astype(BF16)


def _rwkv_kernel(pa_ref, sh_ref, s0_ref, z_ref, mu_ref, w0_ref, a0_ref, wl_ref, kk_ref, ka_ref,
                 rk_ref, g_ref, b_ref, y_ref, sout_ref,
                 s_scr, sb_scr, prev_scr, r_s, w_s, k_s, v_s, kap_s, beta_s, y_s, *, bb_n, lt):
    t = pl.program_id(1)
    bd = _pair_ones()
    bd2 = _quad_ones()

    @pl.when(t == 0)
    def _():
        prev_scr[...] = sh_ref[0]
        for bb in range(bb_n):
            for p in range(N_PAIR):
                s = jnp.concatenate([s0_ref[0, bb, 2 * p], s0_ref[0, bb, 2 * p + 1]], axis=1)
                s_scr[bb, p] = s
                sb_scr[bb, p] = s.astype(BF16)

    w3 = 3 * BRANCH_W
    for bb in range(bb_n):
        pa = pa_ref[bb]
        row = lax.broadcasted_iota(jnp.int32, pa.shape, 0)
        prev = jnp.where(row == 0, prev_scr[bb], pltpu.roll(pa, 1, axis=0))
        prev_scr[bb] = pa[lt - 1:lt, :]
        xs = pa + (prev - pa) * mu_ref[...]
        lo = xs[:, w3:]
        lane = lax.broadcasted_iota(jnp.int32, lo.shape, 1)
        lo = jnp.where(lane < A_DECAY_LORA, jnp.tanh(lo), lo)
        pre = jnp.dot(lo.astype(BF16), wl_ref[...], preferred_element_type=F32)
        for p in range(N_PAIR):
            sl = slice(p * LANES, (p + 1) * LANES)
            r = xs[:, sl]
            k = xs[:, BRANCH_W + p * LANES:BRANCH_W + (p + 1) * LANES]
            v = xs[:, 2 * BRANCH_W + p * LANES:2 * BRANCH_W + (p + 1) * LANES]
            w_log = -_softplus(-(w0_ref[:, sl] + pre[:, sl])) - 0.5
            a = jax.nn.sigmoid(a0_ref[:, sl]
                               + pre[:, BRANCH_W + p * LANES:BRANCH_W + (p + 1) * LANES])
            kk = k * kk_ref[:, sl]
            kk = kk / jnp.maximum(jnp.sqrt(_seg_sum(kk * kk, bd)), 1e-12)
            r_s[bb, :, sl] = r
            w_s[bb, :, sl] = jnp.exp(-jnp.exp(w_log))
            k_s[bb, :, sl] = k * (1.0 + (a - 1.0) * ka_ref[:, sl])
            v_s[bb, :, sl] = v
            kap_s[bb, :, sl] = kk
            beta_s[bb, :, sl] = kk * a

    vi = lax.broadcasted_iota(jnp.int32, (A_HD, LANES), 0)
    li = lax.broadcasted_iota(jnp.int32, (A_HD, LANES), 1)
    diag = jnp.where((li % A_HD) == vi, 1.0, 0.0)
    diag_bf = diag.astype(BF16)
    row8 = lax.broadcasted_iota(jnp.int32, (SUBLANES, LANES), 0)
    lane8 = lax.broadcasted_iota(jnp.int32, (SUBLANES, LANES), 1) % A_HD
    pick = [lane8 - g * SUBLANES == row8 for g in range(A_HD // SUBLANES)]
    chains = [(bb, q) for bb in range(bb_n) for q in range(N_QUAD)]
    groups = [chains[i:i + CHAINS_PER_DOT] for i in range(0, len(chains), CHAINS_PER_DOT)]
    lanes_of = lambda p: slice(p * LANES, (p + 1) * LANES)
    two = 2 * A_HD

    def steps(c, carry):
        rows = pl.ds(pl.multiple_of(c * SUBLANES, SUBLANES), SUBLANES)
        ytile = {(bb, p): jnp.zeros((SUBLANES, LANES), F32)
                 for bb in range(bb_n) for p in range(N_PAIR)}
        for j in range(SUBLANES):
            rowj = lambda ref, bb, p: ref[bb, rows, lanes_of(p)][j:j + 1]
            rowj_bf = lambda ref, bb, p: jnp.broadcast_to(
                rowj(ref, bb, p), (A_HD, LANES)).astype(BF16)
            res1 = []
            for grp in groups:
                lhs = []
                for bb, q in grp:
                    ps, vds = [], []
                    for p in (2 * q, 2 * q + 1):
                        ps.append(sb_scr[bb, p] * rowj_bf(kap_s, bb, p))
                        vds.append(diag_bf * rowj_bf(v_s, bb, p))
                    lhs += [jnp.concatenate(ps, axis=1), jnp.concatenate(vds, axis=1)]
                res1.append(jnp.dot(jnp.concatenate(lhs, axis=0), bd2,
                                    preferred_element_type=F32))
            res2 = []
            for grp, res in zip(groups, res1):
                lhs = []
                for c_i, (bb, q) in enumerate(grp):
                    qs = []
                    for i, p in enumerate((2 * q, 2 * q + 1)):
                        u = res[c_i * two:c_i * two + A_HD, lanes_of(i)]
                        vb = res[c_i * two + A_HD:(c_i + 1) * two, lanes_of(i)]
                        s = (s_scr[bb, p] * rowj(w_s, bb, p) - u * rowj(beta_s, bb, p)
                             + vb * rowj(k_s, bb, p))
                        s_scr[bb, p] = s
                        sb = s.astype(BF16)
                        sb_scr[bb, p] = sb
                        qs.append(sb * rowj_bf(r_s, bb, p))
                    lhs.append(jnp.concatenate(qs, axis=1))
                res2.append(jnp.dot(jnp.concatenate(lhs, axis=0), bd2,
                                    preferred_element_type=F32))
            for grp, res in zip(groups, res2):
                for c_i, (bb, q) in enumerate(grp):
                    for i, p in enumerate((2 * q, 2 * q + 1)):
                        acc = jnp.zeros((SUBLANES, LANES), F32)
                        for g in range(A_HD // SUBLANES):
                            r0 = c_i * A_HD + g * SUBLANES
                            acc = jnp.where(pick[g], res[r0:r0 + SUBLANES, lanes_of(i)], acc)
                        y_row = jnp.sum(acc, axis=0, keepdims=True)
                        ytile[bb, p] = jnp.where(row8 == j, y_row, ytile[bb, p])
        for (bb, p), tile in ytile.items():
            y_s[bb, rows, lanes_of(p)] = tile
        return carry

    lax.fori_loop(0, lt // SUBLANES, steps, 0)

    inv_hd = 1.0 / A_HD
    for bb in range(bb_n):
        for p in range(N_PAIR):
            sl = lanes_of(p)
            y = y_s[bb, :, sl]
            d = y - _seg_sum(y, bd) * inv_hd
            var = _seg_sum(d * d, bd) * inv_hd
            yn = d * lax.rsqrt(var + LN_EPS) * g_ref[:, sl] + b_ref[:, sl]
            bonus = _seg_sum(r_s[bb, :, sl] * k_s[bb, :, sl] * rk_ref[:, sl], bd) * v_s[bb, :, sl]
            y_ref[bb, :, sl] = ((yn + bonus) * z_ref[bb, :, sl]).astype(BF16)

    @pl.when(t == pl.num_programs(1) - 1)
    def _():
        for bb in range(bb_n):
            for p in range(N_PAIR):
                s = s_scr[bb, p]
                sout_ref[bb, 2 * p] = s[:, :A_HD]
                sout_ref[bb, 2 * p + 1] = s[:, A_HD:]


def _rwkv(pa, z, shift, s0, layer, wts):
    bn, ln, _ = pa.shape
    lt = _pick_tile(ln, (128,))
    bb_n = _pick_tile(bn, (4, 2))
    row = lambda n: _full((1, n))
    tok = lambda n, j: pl.BlockSpec((bb_n, lt, n), lambda b, t: (b, t, j))
    scr = lambda: pltpu.VMEM((bb_n, lt, BRANCH_W), F32)
    return pl.pallas_call(
        functools.partial(_rwkv_kernel, bb_n=bb_n, lt=lt),
        grid=(bn // bb_n, ln // lt),
        in_specs=[tok(A_COLS, 0),
                  pl.BlockSpec((1, bb_n, 1, A_COLS), lambda b, t: (layer, b, 0, 0)),
                  pl.BlockSpec((1, bb_n, A_HEADS, A_HD, A_HD), lambda b, t: (layer, b, 0, 0, 0)),
                  tok(BRANCH_W, 0),
                  row(A_COLS), row(BRANCH_W), row(BRANCH_W), _full((LANES, 2 * BRANCH_W)),
                  row(BRANCH_W), row(BRANCH_W), row(BRANCH_W), row(BRANCH_W), row(BRANCH_W)],
        out_specs=[tok(BRANCH_W, 0),
                   pl.BlockSpec((bb_n, A_HEADS, A_HD, A_HD), lambda b, t: (b, 0, 0, 0))],
        out_shape=[jax.ShapeDtypeStruct((bn, ln, BRANCH_W), BF16),
                   jax.ShapeDtypeStruct((bn, A_HEADS, A_HD, A_HD), F32)],
        scratch_shapes=[pltpu.VMEM((bb_n, N_PAIR, A_HD, LANES), F32),
                        pltpu.VMEM((bb_n, N_PAIR, A_HD, LANES), BF16),
                        pltpu.VMEM((bb_n, 1, A_COLS), F32),
                        scr(), scr(), scr(), scr(), scr(), scr(), scr()],
        compiler_params=_params("parallel", "arbitrary"),
        name="rwkv7",
    )(pa, shift, s0, z, wts["a_mu"], wts["a_w0"], wts["a_a0"], wts["a_lora"], wts["a_k_k"],
      wts["a_k_a"], wts["a_r_k"], wts["a_ln_g"], wts["a_ln_b"])


def _rope_swap(x, half):
    w = x.shape[-1]
    if 2 * half == w:
        return pltpu.roll(x, half, axis=1)
    lane = lax.broadcasted_iota(jnp.int32, x.shape, 1)
    return jnp.where(lane % (2 * half) < half,
                     pltpu.roll(x, w - half, axis=1), pltpu.roll(x, half, axis=1))


def _swa_kernel(sink_ref, q_ref, k_ref, v_ref, z_ref, cs_ref, sn_ref, kb_ref, vb_ref,
                y_ref, ko_ref, pk_scr, pv_scr, *, bb_n, lb, nb, hs, pos0):
    n = pl.program_id(1)

    @pl.when(n == 0)
    def _():
        pk_scr[...] = kb_ref[0]
        pv_scr[...] = vb_ref[0]

    cs = cs_ref[...]
    sn = sn_ref[...]
    cs_q = jnp.concatenate([cs] * (B_HEADS * B_HD // LANES), axis=1)
    sn_q = jnp.concatenate([sn] * (B_HEADS * B_HD // LANES), axis=1)
    rows = hs * lb
    r_idx = lax.broadcasted_iota(jnp.int32, (rows, WINDOW + lb), 0)
    a_idx = lax.rem(r_idx, lb)
    c_idx = lax.broadcasted_iota(jnp.int32, (rows, WINDOW + lb), 1)
    key_pos = pos0 + n * lb - WINDOW + c_idx
    mask = (c_idx >= a_idx) & (c_idx <= WINDOW + a_idx) & (key_pos >= 0)
    blk = lax.broadcasted_iota(jnp.int32, (rows, 1), 0) // lb
    sinks = []
    for h0 in range(0, B_HEADS, hs):
        col = jnp.full((rows, 1), sink_ref[h0], F32)
        for i in range(1, hs):
            col = jnp.where(blk == i, sink_ref[h0 + i], col)
        sinks.append(col)
    group = B_HEADS // B_KV_HEADS
    ones_kv = jnp.ones((WINDOW + lb, B_HD), BF16)

    def per_b(bb, carry):
        q = q_ref[bb]
        q = q * cs_q + _rope_swap(q, B_HD // 2) * sn_q
        k = k_ref[bb]
        k = k * cs + _rope_swap(k, B_HD // 2) * sn
        v = v_ref[bb]
        ko_ref[bb] = k
        keys = jnp.concatenate([pk_scr[bb], k], axis=0).astype(BF16)
        vals = jnp.concatenate([pv_scr[bb], v], axis=0).astype(BF16)
        if nb > 1:
            pk_scr[bb] = k
            pv_scr[bb] = v
        scores = []
        for h0 in range(0, B_HEADS, hs):
            g = h0 // group
            qs = jnp.concatenate([q[:, h * B_HD:(h + 1) * B_HD] for h in range(h0, h0 + hs)],
                                 axis=0).astype(BF16)
            scores.append(lax.dot_general(qs, keys[:, g * B_HD:(g + 1) * B_HD], NT_DIMS,
                                          preferred_element_type=F32))
        probs, tails = [], []
        for s, sink in zip(scores, sinks):
            s = jnp.where(mask, s * (B_HD ** -0.5), NEG_INF)
            m = jnp.maximum(jnp.max(s, axis=-1, keepdims=True), sink)
            probs.append(jnp.exp(s - m).astype(BF16))
            tails.append(jnp.exp(sink - m))
        outs = []
        for i, (p, tail) in enumerate(zip(probs, tails)):
            g = (i * hs) // group
            pv = jnp.dot(p, vals[:, g * B_HD:(g + 1) * B_HD], preferred_element_type=F32)
            psum = jnp.dot(p, ones_kv, preferred_element_type=F32)
            o = pv / (psum + tail)
            outs.extend(o[j * lb:(j + 1) * lb] for j in range(hs))
        for j in range(B_HEADS // 2):
            sl = slice(j * LANES, (j + 1) * LANES)
            o = jnp.concatenate([outs[2 * j], outs[2 * j + 1]], axis=1)
            y_ref[bb, :, sl] = (o * z_ref[bb, :, sl]).astype(BF16)
        return carry

    lax.fori_loop(0, bb_n, per_b, 0)


def _swa(pb, z, k_buf, v_buf, layer, sinks, cos_t, sin_t, pos0):
    bn, ln, _ = pb.shape
    lb = WINDOW if ln % WINDOW == 0 else ln
    nb = ln // lb
    bb_n = 1 if nb > 1 else _pick_tile(bn, (8,))
    group = B_HEADS // B_KV_HEADS
    hs = group if group * lb <= LANES else 1
    qw = B_HEADS * B_HD
    tok = lambda n, j: pl.BlockSpec((bb_n, lb, n), lambda b, t: (b, t, j))
    buf = pl.BlockSpec((1, bb_n, WINDOW, LANES), lambda b, t: (layer, b, 0, 0))
    tab = pl.BlockSpec((lb, LANES), lambda b, t: (t, 0))
    return pl.pallas_call(
        functools.partial(_swa_kernel, bb_n=bb_n, lb=lb, nb=nb, hs=hs, pos0=pos0),
        grid=(bn // bb_n, nb),
        in_specs=[pl.BlockSpec(memory_space=pltpu.SMEM),
                  tok(qw, 0), tok(LANES, qw // LANES), tok(LANES, qw // LANES + 1),
                  tok(BRANCH_W, 1), tab, tab, buf, buf],
        out_specs=[tok(BRANCH_W, 0), tok(LANES, 0)],
        out_shape=[jax.ShapeDtypeStruct((bn, ln, BRANCH_W), BF16),
                   jax.ShapeDtypeStruct((bn, ln, LANES), F32)],
        scratch_shapes=[pltpu.VMEM((bb_n, WINDOW, LANES), F32),
                        pltpu.VMEM((bb_n, WINDOW, LANES), F32)],
        compiler_params=_params("parallel", "arbitrary"),
        name="swa",
    )(sinks, pb, pb, pb, z, cos_t, sin_t, k_buf, v_buf)


def _split3(x):
    hi = x.astype(BF16)
    rem = x - hi.astype(F32)
    mid = rem.astype(BF16)
    return hi, mid, (rem - mid.astype(F32)).astype(BF16)


def _dot_sel(x, sel, dims=(((1,), (0,)), ((), ()))):
    return sum(lax.dot_general(p, sel, dims, preferred_element_type=F32) for p in _split3(x))


def _cols_to_rows(x, n_rows):
    r = lax.broadcasted_iota(jnp.int32, (n_rows, LANES), 0)
    c = lax.broadcasted_iota(jnp.int32, (n_rows, LANES), 1)
    sel = jnp.where(r == c, 1.0, 0.0).astype(BF16)
    return sum(lax.dot_general(sel, p, NT_DIMS, preferred_element_type=F32) for p in _split3(x))


def _mlstm_kernel(qk_ref, v_ref, gt_ref, z_ref, cb_ref, cw_ref, cbias_ref, gb_ref,
                  c0_ref, n0_ref, n0h_ref, m0_ref, g_ref, b_ref, eli_ref, eb_ref,
                  y_ref, cout_ref, nout_ref, mout_ref,
                  c_scr, n_scr, nb_scr, m_scr, prev_scr, act_scr, gl_scr, *, bb_n, lt, lc):
    t = pl.program_id(1)

    @pl.when(t == 0)
    def _():
        c_scr[...] = c0_ref[0]
        n_scr[...] = n0_ref[0]
        m_scr[...] = m0_ref[0]
        prev_scr[...] = cb_ref[0]
        for bb in range(bb_n):
            nb = _dot_sel(n0h_ref[0, bb], eli_ref[:C_HEADS, :], TN_DIMS)
            for h in range(C_HEADS):
                nb_scr[bb, h] = nb[:, h * LANES:(h + 1) * LANES]

    row8 = lax.broadcasted_iota(jnp.int32, (SUBLANES, C_CONV_COLS), 0)
    lane = lax.broadcasted_iota(jnp.int32, (lt, LANES), 1)
    qi = lax.broadcasted_iota(jnp.int32, (lc, lc), 0)
    ki = lax.broadcasted_iota(jnp.int32, (lc, lc), 1)
    tril = ki <= qi
    tril_bf = jnp.where(tril, 1.0, 0.0).astype(BF16)
    ones_bf = jnp.ones((lc, LANES), BF16)
    qkw = C_HEADS * C_QK

    def per_b(bb, carry):
        u = qk_ref[bb]
        prev8 = prev_scr[bb]
        conv = cbias_ref[...] + u * cw_ref[C_CONV - 1:C_CONV, :]
        for s in range(1, C_CONV):
            sh = pltpu.roll(u, s, axis=0)
            head = jnp.where(row8 < s, pltpu.roll(prev8, s, axis=0), sh[:SUBLANES])
            sh = head if lt == SUBLANES else jnp.concatenate([head, sh[SUBLANES:]], axis=0)
            conv = conv + sh * cw_ref[C_CONV - 1 - s:C_CONV - s, :]
        act_scr[...] = conv * jax.nn.sigmoid(conv)
        prev_scr[bb] = u[lt - SUBLANES:, :]
        g = gt_ref[bb] + gb_ref[...]
        gl_scr[...] = jnp.where(lane < C_HEADS, g, -_softplus(-g))

        def chunk(r0):
            glc = gl_scr[pl.ds(r0, lc), :]
            cum = sum(jnp.dot(tril_bf, p, preferred_element_type=F32) for p in _split3(glc))
            glr = _cols_to_rows(glc, C_HEADS)
            cur = _cols_to_rows(cum, 2 * C_HEADS)
            li_x = _dot_sel(glc, eli_ref[...])
            b_x = _dot_sel(cum, eb_ref[...])
            actc = act_scr[pl.ds(r0, lc), :]
            vc = v_ref[bb, pl.ds(r0, lc), :]
            zc = z_ref[bb, pl.ds(r0, lc), :]
            heads = range(C_HEADS)
            vsl = lambda h: slice(h * C_V, (h + 1) * C_V)
            qbs, vs, qk, qc, qn, kws, gate = [], [], [], [], [], [], []
            for h in heads:
                q = actc[:, h * C_QK:(h + 1) * C_QK]
                k = actc[:, qkw + h * C_QK:qkw + (h + 1) * C_QK] * (C_QK ** -0.5)
                qb = q.astype(BF16)
                vs.append(vc[:, vsl(h)].astype(BF16))
                qk.append(lax.dot_general(qb, k.astype(BF16), NT_DIMS,
                                          preferred_element_type=F32))
                qc.append(jnp.dot(qb, c_scr[bb, h].astype(BF16), preferred_element_type=F32))
                qn.append(jnp.dot(qb, nb_scr[bb, h].astype(BF16), preferred_element_type=F32))
                b_q = b_x[:, vsl(h)]
                li_row = glr[h:h + 1, :]
                b_row = cur[C_HEADS + h:C_HEADS + h + 1, :]
                m_s = m_scr[bb, h]
                dmat = jnp.where(tril, b_q[:, :lc] - b_row + li_row, NEG_INF)
                inter = b_q + m_s
                mt = jnp.maximum(inter, jnp.max(dmat, axis=1, keepdims=True))
                b_last = b_row[:, lc - 1:lc]
                gl_row = b_last - b_row + li_row
                gl_col = b_last - b_q[:, :C_QK] + li_x[:, h * LANES:h * LANES + C_QK]
                m_new = jnp.maximum(b_last + m_s[:, :1], jnp.max(gl_row, axis=1, keepdims=True))
                kws.append(k * jnp.exp(gl_col - m_new))
                gate.append((jnp.exp(dmat - mt[:, :lc]), jnp.exp(inter - mt), mt, m_new,
                             jnp.exp(b_last + m_s[:, :1] - m_new)))
            nums, dens = [], []
            for h in heads:
                wts, sc, _, _, _ = gate[h]
                a = (qk[h] * wts).astype(BF16)
                nums.append(jnp.dot(a, vs[h], preferred_element_type=F32) + sc * qc[h])
                dens.append(jnp.dot(a, ones_bf, preferred_element_type=F32) + sc * qn[h])
            for h in heads:
                _, _, _, m_new, s_old = gate[h]
                upd = lax.dot_general(kws[h].astype(BF16),
                                      jnp.concatenate([vs[h], ones_bf], axis=1), TN_DIMS,
                                      preferred_element_type=F32)
                c_scr[bb, h] = s_old * c_scr[bb, h] + upd[:, :C_V]
                nb_scr[bb, h] = s_old * nb_scr[bb, h] + upd[:, C_V:]
                n_scr[bb, h] = s_old * n_scr[bb, h] + jnp.sum(kws[h], axis=0, keepdims=True)
                m_scr[bb, h] = jnp.broadcast_to(m_new, (1, LANES))
            for h in heads:
                hh = nums[h] / jnp.maximum(jnp.abs(dens[h]), jnp.exp(-gate[h][2]))
                yn = _ln_lanes(hh, g_ref[:, vsl(h)], b_ref[:, vsl(h)])
                y_ref[bb, pl.ds(r0, lc), vsl(h)] = (yn * zc[:, vsl(h)]).astype(BF16)

        if lt == lc:
            chunk(0)
        else:
            def body(c, cc):
                chunk(pl.multiple_of(c * lc, lc))
                return cc
            lax.fori_loop(0, lt // lc, body, 0)
        return carry

    lax.fori_loop(0, bb_n, per_b, 0)

    @pl.when(t == pl.num_programs(1) - 1)
    def _():
        cout_ref[...] = c_scr[...]
        nout_ref[...] = n_scr[...]
        mout_ref[...] = m_scr[...]


def _mlstm(qk, v, gates, z, conv8, c0, n0, m0, layer, wts):
    bn, ln, _ = qk.shape
    lc = CHUNK if ln % CHUNK == 0 else ln
    lt = _pick_tile(ln, (256,))
    bb_n = 1 if ln > lt or bn < SUBLANES else SUBLANES
    tok = lambda n, j: pl.BlockSpec((bb_n, lt, n), lambda b, t: (b, t, j))
    st_in = lambda *s: pl.BlockSpec((1, bb_n) + s, lambda b, t: (layer, b) + (0,) * len(s))
    st_out = lambda *s: pl.BlockSpec((bb_n,) + s, lambda b, t: (b,) + (0,) * len(s))
    row = lambda n: _full((1, n))
    col = lax.broadcasted_iota(jnp.int32, (LANES, C_HEADS * LANES), 0)
    blk = lax.broadcasted_iota(jnp.int32, (LANES, C_HEADS * LANES), 1) // LANES
    e_li = (col == blk).astype(BF16)
    e_b = (col == blk + C_HEADS).astype(BF16)
    return pl.pallas_call(
        functools.partial(_mlstm_kernel, bb_n=bb_n, lt=lt, lc=lc),
        grid=(bn // bb_n, ln // lt),
        in_specs=[tok(C_CONV_COLS, 0), tok(BRANCH_W, 0), tok(LANES, 0), tok(BRANCH_W, 2),
                  st_in(SUBLANES, C_CONV_COLS), _full((C_CONV, C_CONV_COLS)), row(C_CONV_COLS),
                  row(LANES),
                  st_in(C_HEADS, C_QK, C_V), st_in(C_HEADS, 1, C_QK), st_in(C_HEADS, C_QK),
                  st_in(C_HEADS, 1, LANES),
                  row(BRANCH_W), row(BRANCH_W),
                  _full((LANES, C_HEADS * LANES)), _full((LANES, C_HEADS * LANES))],
        out_specs=[tok(BRANCH_W, 0), st_out(C_HEADS, C_QK, C_V), st_out(C_HEADS, 1, C_QK),
                   st_out(C_HEADS, 1, LANES)],
        out_shape=[jax.ShapeDtypeStruct((bn, ln, BRANCH_W), BF16),
                   jax.ShapeDtypeStruct((bn, C_HEADS, C_QK, C_V), F32),
                   jax.ShapeDtypeStruct((bn, C_HEADS, 1, C_QK), F32),
                   jax.ShapeDtypeStruct((bn, C_HEADS, 1, LANES), F32)],
        scratch_shapes=[pltpu.VMEM((bb_n, C_HEADS, C_QK, C_V), F32),
                        pltpu.VMEM((bb_n, C_HEADS, 1, C_QK), F32),
                        pltpu.VMEM((bb_n, C_HEADS, C_QK, LANES), F32),
                        pltpu.VMEM((bb_n, C_HEADS, 1, LANES), F32),
                        pltpu.VMEM((bb_n, SUBLANES, C_CONV_COLS), F32),
                        pltpu.VMEM((lt, C_CONV_COLS), F32),
                        pltpu.VMEM((lt, LANES), F32)],
        compiler_params=_params("parallel", "arbitrary"),
        name="mlstm",
    )(qk, v, gates, z, conv8, wts["c_conv_w"], wts["c_conv_b"], wts["c_gate_b"],
      c0, n0, n0[:, :, :, 0, :], m0, wts["c_ln_g"], wts["c_ln_b"], e_li, e_b)


def _ret_kernel(q_ref, k_ref, v_ref, z_ref, cs_ref, sn_ref, dm_ref, qd_ref, kd_ref, cd_ref,
                s0_ref, g_ref, b_ref, y_ref, sout_ref, s_scr, *, bb_n, lt, lc):
    t = pl.program_id(1)

    @pl.when(t == 0)
    def _():
        s_scr[...] = s0_ref[0]

    heads = range(D_HEADS)
    lanes_of = lambda h: slice(h * LANES, (h + 1) * LANES)

    def per_b(bb, carry):
        def chunk(r0):
            rows = pl.ds(r0, lc)
            cs = cs_ref[rows, :]
            sn = sn_ref[rows, :]
            qs, ks, vs, inner, cross = [], [], [], [], []
            for h in heads:
                q = q_ref[bb, rows, lanes_of(h)]
                k = k_ref[bb, rows, lanes_of(h)]
                q = (q * cs + _rope_swap(q, D_QK // 2) * sn).astype(BF16)
                k = (k * cs + _rope_swap(k, D_QK // 2) * sn) * (D_QK ** -0.5)
                qs.append(q)
                ks.append(k)
                vs.append(v_ref[bb, rows, lanes_of(h)].astype(BF16))
                inner.append(lax.dot_general(q, k.astype(BF16), NT_DIMS,
                                             preferred_element_type=F32))
                cross.append(jnp.dot(q, s_scr[bb, h].astype(BF16), preferred_element_type=F32))
            outs = []
            for h in heads:
                a = (inner[h] * dm_ref[h]).astype(BF16)
                outs.append(jnp.dot(a, vs[h], preferred_element_type=F32) + cross[h] * qd_ref[h])
            for h in heads:
                s_scr[bb, h] = cd_ref[h] * s_scr[bb, h] + lax.dot_general(
                    (ks[h] * kd_ref[h]).astype(BF16), vs[h], TN_DIMS, preferred_element_type=F32)
            for h in heads:
                yn = _ln_lanes(outs[h], g_ref[:, lanes_of(h)], b_ref[:, lanes_of(h)])
                y_ref[bb, rows, lanes_of(h)] = (yn * z_ref[bb, rows, lanes_of(h)]).astype(BF16)

        if lt == lc:
            chunk(0)
        else:
            def body(c, cc):
                chunk(pl.multiple_of(c * lc, lc))
                return cc
            lax.fori_loop(0, lt // lc, body, 0)
        return carry

    lax.fori_loop(0, bb_n, per_b, 0)

    @pl.when(t == pl.num_programs(1) - 1)
    def _():
        sout_ref[...] = s_scr[...]


def _retention(pd, z, s0, layer, wts, cos_t, sin_t):
    bn, ln, _ = pd.shape
    lc = CHUNK if ln % CHUNK == 0 else ln
    lt = _pick_tile(ln, (256,))
    bb_n = 1 if ln > lt or bn < SUBLANES else SUBLANES
    hw = D_HEADS * D_QK
    tok = lambda j: pl.BlockSpec((bb_n, lt, hw), lambda b, t: (b, t, j))
    tab = pl.BlockSpec((lt, LANES), lambda b, t: (t, 0))
    dec = wts["d_tables"][lc]
    return pl.pallas_call(
        functools.partial(_ret_kernel, bb_n=bb_n, lt=lt, lc=lc),
        grid=(bn // bb_n, ln // lt),
        in_specs=[tok(0), tok(1), tok(2), tok(3), tab, tab,
                  _full((D_HEADS, lc, lc)), _full((D_HEADS, lc, LANES)),
                  _full((D_HEADS, lc, LANES)), _full((D_HEADS, 1, LANES)),
                  pl.BlockSpec((1, bb_n, D_HEADS, D_QK, D_V), lambda b, t: (layer, b, 0, 0, 0)),
                  _full((1, BRANCH_W)), _full((1, BRANCH_W))],
        out_specs=[tok(0),
                   pl.BlockSpec((bb_n, D_HEADS, D_QK, D_V), lambda b, t: (b, 0, 0, 0))],
        out_shape=[jax.ShapeDtypeStruct((bn, ln, BRANCH_W), BF16),
                   jax.ShapeDtypeStruct((bn, D_HEADS, D_QK, D_V), F32)],
        scratch_shapes=[pltpu.VMEM((bb_n, D_HEADS, D_QK, D_V), F32)],
        compiler_params=_params("parallel", "arbitrary"),
        name="retention",
    )(pd, pd, pd, z, cos_t, sin_t, dec["decay_mat"], dec["q_dec"], dec["k_dec"], dec["c_dec"],
      s0, wts["d_ln_g"], wts["d_ln_b"])


def _rope_tables(pos, d):
    inv = ROPE_THETA ** (-jnp.arange(0, d, 2, dtype=F32) / d)
    ang = pos.astype(F32)[:, None] * inv[None, :]
    cos = jnp.cos(ang)
    sin = jnp.sin(ang)
    reps = LANES // d
    return (jnp.tile(jnp.concatenate([cos, cos], -1), (1, reps)),
            jnp.tile(jnp.concatenate([-sin, sin], -1), (1, reps)))


def _retention_tables(lc):
    log_gamma = jnp.log1p(-jnp.exp2(-5.0 - jnp.arange(D_HEADS, dtype=F32)))
    idx = jnp.arange(lc, dtype=F32)
    rel = idx[:, None] - idx[None, :]
    decay_mat = jnp.where(rel >= 0, jnp.exp(jnp.maximum(rel, 0.0) * log_gamma[:, None, None]), 0.0)
    q_dec = jnp.exp((idx + 1.0)[None, :] * log_gamma[:, None])
    k_dec = jnp.exp((lc - 1.0 - idx)[None, :] * log_gamma[:, None])
    c_dec = jnp.exp(lc * log_gamma)
    bcast = lambda a: jnp.broadcast_to(a[..., None], a.shape + (LANES,))
    return {"decay_mat": decay_mat, "q_dec": bcast(q_dec), "k_dec": bcast(k_dec),
            "c_dec": bcast(c_dec[:, None])}


def _layer_weights(l, w_in, a_mu, a_w0, a_w_up, a_a0, a_a_up, a_k_k, a_k_a, a_r_k, a_ln_g, a_ln_b,
                   b_sinks, c_conv_w, c_conv_b, c_i_bias, c_f_bias, c_ln_g, c_ln_b,
                   d_ln_g, d_ln_b, w_branch, w_out, ln_g, ln_b, chunk_lens):
    o1 = A_COLS
    o2 = o1 + B_COLS
    o3 = o2 + C_COLS
    o4 = MIX_COLS
    o5 = o4 + N_BRANCH * BRANCH_W
    og = o2 + C_CONV_COLS + C_HEADS * C_V
    wl = w_in[l]
    seg = lambda a, b: wl[:, a:b].astype(BF16)
    row = lambda a: a[l].reshape(1, -1)
    zeros = jnp.zeros((A_DECAY_LORA, BRANCH_W), F32)
    lora = jnp.concatenate([jnp.concatenate([a_w_up[l], zeros], 1),
                            jnp.concatenate([zeros, a_a_up[l]], 1)], 0).astype(BF16)
    gate_b = jnp.pad(jnp.concatenate([c_i_bias[l], c_f_bias[l]]), (0, LANES - 2 * C_HEADS))
    return {
        "w_a": seg(0, o1), "w_b": seg(o1, o2), "w_cqk": seg(o2, o2 + C_CONV_COLS),
        "w_cv": seg(o2 + C_CONV_COLS, og),
        "w_cg": jnp.pad(wl[:, og:o3], ((0, 0), (0, LANES - 2 * C_HEADS))).astype(BF16),
        "w_d": seg(o3, o4), "w_z": seg(o4, o5), "w_g": seg(o5, wl.shape[1]),
        "a_mu": row(a_mu), "a_w0": row(a_w0), "a_a0": row(a_a0), "a_lora": lora,
        "a_k_k": row(a_k_k), "a_k_a": row(a_k_a), "a_r_k": row(a_r_k),
        "a_ln_g": row(a_ln_g), "a_ln_b": row(a_ln_b),
        "b_sinks": b_sinks[l],
        "c_conv_w": c_conv_w[l], "c_conv_b": row(c_conv_b), "c_gate_b": gate_b.reshape(1, LANES),
        "c_ln_g": row(c_ln_g), "c_ln_b": row(c_ln_b),
        "d_ln_g": row(d_ln_g), "d_ln_b": row(d_ln_b),
        "d_tables": {lc: _retention_tables(lc) for lc in chunk_lens},
        "w_branch": w_branch[l].astype(BF16), "w_out": w_out[l].astype(BF16),
        "ln_g": row(ln_g), "ln_b": row(ln_b),
    }


def _hybrid_layer(xf, xb, pos0, l, st, wts, tabs):
    bn, ln, _ = xf.shape
    m = bn * ln
    s_a, shift_a, k_buf, v_buf, c_m, n_m, m_m, conv_m, s_d = st
    tok = lambda a: a.reshape(bn, ln, a.shape[-1])
    pa = tok(_matmul(xb, wts["w_a"], name="proj_a"))
    pb = tok(_matmul(xb, wts["w_b"], name="proj_b"))
    pcqk = tok(_matmul(xb, wts["w_cqk"], name="proj_cqk"))
    pcv = tok(_matmul(xb, wts["w_cv"], name="proj_cv"))
    pcg = tok(_matmul(xb, wts["w_cg"], name="proj_cg"))
    pd = tok(_matmul(xb, wts["w_d"], name="proj_d"))
    z = tok(_matmul(xb, wts["w_z"], act="silu", name="proj_z"))
    gate = _matmul(xb, wts["w_g"], act="sigmoid", name="proj_gate")

    y_a, s_a_new = _rwkv(pa, z, shift_a, s_a, l, wts)
    shift_new = pa[:, -1]

    y_b, k_rot = _swa(pb, z, k_buf, v_buf, l, wts["b_sinks"], tabs["cos64"], tabs["sin64"], pos0)
    qw = B_HEADS * B_HD
    kvw = B_KV_HEADS * B_HD
    kv_shape = (bn, WINDOW, B_KV_HEADS, B_HD)
    k_new = jnp.concatenate([k_buf[l], k_rot], 1)[:, -WINDOW:].reshape(kv_shape)
    v_new = jnp.concatenate([v_buf[l], pb[..., qw + kvw:]], 1)[:, -WINDOW:].reshape(kv_shape)

    y_c, c_new, n_new, m_new = _mlstm(pcqk, pcv, pcg, z, conv_m, c_m, n_m, m_m, l, wts)
    conv_new = jnp.concatenate([conv_m[l, :, SUBLANES - (C_CONV - 1):], pcqk], 1)[:, -(C_CONV - 1):]

    y_d, s_d_new = _retention(pd, z, s_d, l, wts, tabs["cos128"], tabs["sin128"])

    flat = lambda a: a.reshape(m, a.shape[-1])
    merged = _merge([flat(y_a), flat(y_b), flat(y_c), flat(y_d)], gate, wts["w_branch"])
    xf_new, xb_new = _out_ln(merged, wts["w_out"], flat(xf), wts["ln_g"], wts["ln_b"])
    new = (s_a_new, shift_new, k_new, v_new, c_new, n_new[:, :, 0], m_new[:, :, 0, 0],
           conv_new, s_d_new)
    return xf_new.reshape(bn, ln, D_MODEL), xb_new, new


def _trunk(x, pos0, states, layer_wts):
    bn, ln, _ = x.shape
    s_a, shift_a, k_buf, v_buf, c_m, n_m, m_m, conv_m, s_d = states
    depth = s_a.shape[0]
    st = (s_a,
          shift_a[:, :, None, :],
          k_buf.reshape(depth, bn, WINDOW, LANES),
          v_buf.reshape(depth, bn, WINDOW, LANES),
          c_m,
          n_m[:, :, :, None, :],
          jnp.broadcast_to(m_m[..., None, None], m_m.shape + (1, LANES)),
          jnp.pad(conv_m, ((0, 0), (0, 0), (SUBLANES - (C_CONV - 1), 0), (0, 0))),
          s_d)
    pos = pos0 + jnp.arange(ln)
    cos64, sin64 = _rope_tables(pos, B_HD)
    cos128, sin128 = _rope_tables(pos, D_QK)
    tabs = {"cos64": cos64, "sin64": sin64, "cos128": cos128, "sin128": sin128}
    xb = x.reshape(bn * ln, D_MODEL).astype(BF16)
    new = []
    for l in range(depth):
        x, xb, st_new = _hybrid_layer(x, xb, pos0, l, st, layer_wts[l], tabs)
        new.append(st_new)
    stacked = tuple(jnp.stack([s[j] for s in new]) for j in range(len(states)))
    return x, stacked


def kernel(x_prompt, x_sample, state_rwkv_S, state_rwkv_shift, cache_swa_k, cache_swa_v,
           state_mlstm_C, state_mlstm_n, state_mlstm_m, state_mlstm_conv, state_ret_S,
           w_in, a_mu, a_w0, a_w_up, a_a0, a_a_up, a_k_k, a_k_a, a_r_k, a_ln_g, a_ln_b,
           b_sinks, c_conv_w, c_conv_b, c_i_bias, c_f_bias, c_ln_g, c_ln_b,
           d_ln_g, d_ln_b, w_branch, w_out, ln_g, ln_b):
    weights = (w_in, a_mu, a_w0, a_w_up, a_a0, a_a_up, a_k_k, a_k_a, a_r_k, a_ln_g, a_ln_b,
               b_sinks, c_conv_w, c_conv_b, c_i_bias, c_f_bias, c_ln_g, c_ln_b,
               d_ln_g, d_ln_b, w_branch, w_out, ln_g, ln_b)
    sample_states = (state_rwkv_S, state_rwkv_shift, cache_swa_k, cache_swa_v,
                     state_mlstm_C, state_mlstm_n, state_mlstm_m, state_mlstm_conv, state_ret_S)
    depth = w_in.shape[0]
    chunk_len = lambda ln: CHUNK if ln % CHUNK == 0 else ln
    chunk_lens = {chunk_len(x_prompt.shape[1]), chunk_len(x_sample.shape[1])}
    layer_wts = [_layer_weights(l, *weights, chunk_lens) for l in range(depth)]
    n_prompt = x_prompt.shape[0]
    zero_states = tuple(jnp.zeros((depth, n_prompt) + s.shape[2:], s.dtype) for s in sample_states)
    y_prompt, p = _trunk(x_prompt, 0, zero_states, layer_wts)
    y_sample, s = _trunk(x_sample, PAST_LEN, sample_states, layer_wts)
    return (y_prompt, y_sample, p[0], s[0], p[1], s[1], p[2], s[2], p[3], s[3], p[4], s[4],
            p[5], s[5], p[6], s[6], p[7], s[7], p[8], s[8])
```

```python
import functools

import jax
import jax.numpy as jnp
from jax import lax
from jax.experimental import pallas as pl
from jax.experimental.pallas import tpu as pltpu

F32 = jnp.float32
BF16 = jnp.bfloat16

D_MODEL = 2048
DEPTH = 2
PAST_LEN = 8192
N_BRANCH = 4
BRANCH_W = D_MODEL // 2
A_HD = 64
A_HEADS = BRANCH_W // A_HD
A_DECAY_LORA = 64
A_AAA_LORA = 64
B_HD = 64
B_HEADS = BRANCH_W // B_HD
B_KV_HEADS = B_HEADS // 8
WINDOW = 128
ROPE_THETA = 10000.0
C_HEADS = 8
C_V = BRANCH_W // C_HEADS
C_QK = C_V // 2
C_CONV = 4
D_HEADS = 8
D_QK = BRANCH_W // D_HEADS
D_V = BRANCH_W // D_HEADS
CHUNK = 64
LN_EPS = 1e-5
NEG_INF = -1e30
ALPHA = (2 * DEPTH) ** 0.25

A_COLS = 3 * BRANCH_W + A_DECAY_LORA + A_AAA_LORA
B_COLS = (B_HEADS + 2 * B_KV_HEADS) * B_HD
C_CONV_COLS = 2 * C_HEADS * C_QK
C_COLS = C_CONV_COLS + C_HEADS * C_V + 2 * C_HEADS
D_COLS = D_HEADS * (2 * D_QK + D_V)
MIX_COLS = A_COLS + B_COLS + C_COLS + D_COLS

LANES = 128
SUBLANES = 8
VMEM_LIMIT_BYTES = 56 * 1024 * 1024

NT_DIMS = (((1,), (1,)), ((), ()))
TN_DIMS = (((0,), (0,)), ((), ()))


def _params(*sem):
    return pltpu.CompilerParams(dimension_semantics=sem, vmem_limit_bytes=VMEM_LIMIT_BYTES)


def _full(shape):
    n = len(shape)
    return pl.BlockSpec(shape, lambda *_: (0,) * n)


def _softplus(x):
    return jnp.maximum(x, 0.0) + jnp.log1p(jnp.exp(-jnp.abs(x)))


def _split2(x):
    hi = x.astype(BF16)
    lo = (x - hi.astype(F32)).astype(BF16)
    return hi, lo


def _pair_ones():
    r = lax.broadcasted_iota(jnp.int32, (LANES, LANES), 0) // A_HD
    c = lax.broadcasted_iota(jnp.int32, (LANES, LANES), 1) // A_HD
    return jnp.where(r == c, 1.0, 0.0).astype(BF16)


def _seg_sum(x, bd):
    hi, lo = _split2(x)
    return (jnp.dot(hi, bd, preferred_element_type=F32)
            + jnp.dot(lo, bd, preferred_element_type=F32))


def _ln_lanes(x, g, b):
    mu = jnp.mean(x, axis=-1, keepdims=True)
    d = x - mu
    var = jnp.mean(d * d, axis=-1, keepdims=True)
    return d * lax.rsqrt(var + LN_EPS) * g + b


def _mm_kernel(x_ref, w_ref, o_ref, *, act):
    acc = jnp.dot(x_ref[...], w_ref[...], preferred_element_type=F32)
    if act == "silu":
        acc = acc * jax.nn.sigmoid(acc)
    elif act == "sigmoid":
        acc = jax.nn.sigmoid(acc)
    o_ref[...] = acc.astype(o_ref.dtype)


def _pick_tile(n, candidates):
    for c in candidates:
        if n % c == 0:
            return c
    return n


def _matmul(x, w, act=None, name="proj"):
    m, k = x.shape
    n = w.shape[1]
    tm = _pick_tile(m, (1024,))
    tn = _pick_tile(n, (1024, 640, 512, 256, 128))
    return pl.pallas_call(
        functools.partial(_mm_kernel, act=act),
        grid=(m // tm, n // tn),
        in_specs=[pl.BlockSpec((tm, k), lambda i, j: (i, 0)),
                  pl.BlockSpec((k, tn), lambda i, j: (0, j))],
        out_specs=pl.BlockSpec((tm, tn), lambda i, j: (i, j)),
        out_shape=jax.ShapeDtypeStruct((m, n), F32),
        compiler_params=_params("parallel", "arbitrary"),
        name=name,
    )(x, w)


def _merge_kernel(ya_ref, yb_ref, yc_ref, yd_ref, g_ref, w_ref, o_ref, acc_ref):
    i = pl.program_id(1)

    @pl.when(i == 0)
    def _():
        acc_ref[...] = jnp.zeros_like(acc_ref)

    for b, y_ref in enumerate((ya_ref, yb_ref, yc_ref, yd_ref)):
        @pl.when(i == b)
        def _(y_ref=y_ref):
            acc_ref[...] += g_ref[...] * jnp.dot(y_ref[...], w_ref[0],
                                                 preferred_element_type=F32)

    @pl.when(i == N_BRANCH - 1)
    def _():
        o_ref[...] = acc_ref[...].astype(o_ref.dtype)


def _merge(ys, gate, w_branch):
    m = gate.shape[0]
    tm = _pick_tile(m, (512,))
    y_spec = pl.BlockSpec((tm, BRANCH_W), lambda i, j: (i, 0))
    return pl.pallas_call(
        _merge_kernel,
        grid=(m // tm, N_BRANCH),
        in_specs=[y_spec, y_spec, y_spec, y_spec,
                  pl.BlockSpec((tm, D_MODEL), lambda i, j: (i, j)),
                  pl.BlockSpec((1, BRANCH_W, D_MODEL), lambda i, j: (j, 0, 0))],
        out_specs=pl.BlockSpec((tm, D_MODEL), lambda i, j: (i, 0)),
        out_shape=jax.ShapeDtypeStruct((m, D_MODEL), BF16),
        scratch_shapes=[pltpu.VMEM((tm, D_MODEL), F32)],
        compiler_params=_params("parallel", "arbitrary"),
        name="merge",
    )(*ys, gate, w_branch)


def _outln_kernel(m_ref, w_ref, x_ref, g_ref, b_ref, of_ref, ob_ref):
    out = jnp.dot(m_ref[...], w_ref[...], preferred_element_type=F32)
    y = _ln_lanes(ALPHA * x_ref[...] + out, g_ref[...], b_ref[...])
    of_ref[...] = y
    ob_ref[...] = y.astype(BF16)


def _out_ln(merged, w_out, x, g, b):
    m = x.shape[0]
    tm = _pick_tile(m, (512,))
    row = pl.BlockSpec((tm, D_MODEL), lambda i: (i, 0))
    return pl.pallas_call(
        _outln_kernel,
        grid=(m // tm,),
        in_specs=[row, _full((D_MODEL, D_MODEL)), row, _full((1, D_MODEL)), _full((1, D_MODEL))],
        out_specs=[row, row],
        out_shape=[jax.ShapeDtypeStruct((m, D_MODEL), F32),
                   jax.ShapeDtypeStruct((m, D_MODEL), BF16)],
        compiler_params=_params("parallel"),
        name="out_ln",
    )(merged, w_out, x, g, b)


N_PAIR = A_HEADS // 2
N_QUAD = N_PAIR // 2
CHAINS_PER_DOT = 8


def _quad_ones():
    n = 2 * LANES
    r = lax.broadcasted_iota(jnp.int32, (n, n), 0) // A_HD
    c = lax.broadcasted_iota(jnp.int32, (n, n), 1) // A_HD
    return jnp.where(r == c, 1.0, 0.0).astype(BF16)


def _rwkv_kernel(pa_ref, sh_ref, s0_ref, z_ref, mu_ref, w0_ref, a0_ref, wl_ref, kk_ref, ka_ref,
                 rk_ref, g_ref, b_ref, y_ref, sout_ref,
                 s_scr, sb_scr, prev_scr, r_s, w_s, k_s, v_s, kap_s, beta_s, y_s, *, bb_n, lt):
    t = pl.program_id(1)
    bd = _pair_ones()
    bd2 = _quad_ones()

    @pl.when(t == 0)
    def _():
        prev_scr[...] = sh_ref[0]
        for bb in range(bb_n):
            for p in range(N_PAIR):
                s = jnp.concatenate([s0_ref[0, bb, 2 * p], s0_ref[0, bb, 2 * p + 1]], axis=1)
                s_scr[bb, p] = s
                sb_scr[bb, p] = s.astype(BF16)

    w3 = 3 * BRANCH_W
    for bb in range(bb_n):
        pa = pa_ref[bb]
        row = lax.broadcasted_iota(jnp.int32, pa.shape, 0)
        prev = jnp.where(row == 0, prev_scr[bb], pltpu.roll(pa, 1, axis=0))
        prev_scr[bb] = pa[lt - 1:lt, :]
        xs = pa + (prev - pa) * mu_ref[...]
        lo = xs[:, w3:]
        lane = lax.broadcasted_iota(jnp.int32, lo.shape, 1)
        lo = jnp.where(lane < A_DECAY_LORA, jnp.tanh(lo), lo)
        pre = jnp.dot(lo.astype(BF16), wl_ref[...], preferred_element_type=F32)
        for p in range(N_PAIR):
            sl = slice(p * LANES, (p + 1) * LANES)
            r = xs[:, sl]
            k = xs[:, BRANCH_W + p * LANES:BRANCH_W + (p + 1) * LANES]
            v = xs[:, 2 * BRANCH_W + p * LANES:2 * BRANCH_W + (p + 1) * LANES]
            w_log = -_softplus(-(w0_ref[:, sl] + pre[:, sl])) - 0.5
            a = jax.nn.sigmoid(a0_ref[:, sl]
                               + pre[:, BRANCH_W + p * LANES:BRANCH_W + (p + 1) * LANES])
            kk = k * kk_ref[:, sl]
            kk = kk / jnp.maximum(jnp.sqrt(_seg_sum(kk * kk, bd)), 1e-12)
            r_s[bb, :, sl] = r
            w_s[bb, :, sl] = jnp.exp(-jnp.exp(w_log))
            k_s[bb, :, sl] = k * (1.0 + (a - 1.0) * ka_ref[:, sl])
            v_s[bb, :, sl] = v
            kap_s[bb, :, sl] = kk
            beta_s[bb, :, sl] = kk * a

    vi = lax.broadcasted_iota(jnp.int32, (A_HD, LANES), 0)
    li = lax.broadcasted_iota(jnp.int32, (A_HD, LANES), 1)
    diag = jnp.where((li % A_HD) == vi, 1.0, 0.0)
    diag_bf = diag.astype(BF16)
    row8 = lax.broadcasted_iota(jnp.int32, (SUBLANES, LANES), 0)
    lane8 = lax.broadcasted_iota(jnp.int32, (SUBLANES, LANES), 1) % A_HD
    pick = [lane8 - g * SUBLANES == row8 for g in range(A_HD // SUBLANES)]
    chains = [(bb, q) for bb in range(bb_n) for q in range(N_QUAD)]
    groups = [chains[i:i + CHAINS_PER_DOT] for i in range(0, len(chains), CHAINS_PER_DOT)]
    lanes_of = lambda p: slice(p * LANES, (p + 1) * LANES)
    two = 2 * A_HD

    def steps(c, carry):
        rows = pl.ds(pl.multiple_of(c * SUBLANES, SUBLANES), SUBLANES)
        ytile = {(bb, p): jnp.zeros((SUBLANES, LANES), F32)
                 for bb in range(bb_n) for p in range(N_PAIR)}
        for j in range(SUBLANES):
            rowj = lambda ref, bb, p: ref[bb, rows, lanes_of(p)][j:j + 1]
            rowj_bf = lambda ref, bb, p: jnp.broadcast_to(
                rowj(ref, bb, p), (A_HD, LANES)).astype(BF16)
            res1 = []
            for grp in groups:
                lhs = []
                for bb, q in grp:
                    ps, vds = [], []
                    for p in (2 * q, 2 * q + 1):
                        ps.append(sb_scr[bb, p] * rowj_bf(kap_s, bb, p))
                        vds.append(diag_bf * rowj_bf(v_s, bb, p))
                    lhs += [jnp.concatenate(ps, axis=1), jnp.concatenate(vds, axis=1)]
                res1.append(jnp.dot(jnp.concatenate(lhs, axis=0), bd2,
                                    preferred_element_type=F32))
            res2 = []
            for grp, res in zip(groups, res1):
                lhs = []
                for c_i, (bb, q) in enumerate(grp):
                    qs = []
                    for i, p in enumerate((2 * q, 2 * q + 1)):
                        u = res[c_i * two:c_i * two + A_HD, lanes_of(i)]
                        vb = res[c_i * two + A_HD:(c_i + 1) * two, lanes_of(i)]
                        s = (s_scr[bb, p] * rowj(w_s, bb, p) - u * rowj(beta_s, bb, p)
                             + vb * rowj(k_s, bb, p))
                        s_scr[bb, p] = s
                        sb = s.astype(BF16)
                        sb_scr[bb, p] = sb
                        qs.append(sb * rowj_bf(r_s, bb, p))
                    lhs.append(jnp.concatenate(qs, axis=1))
                res2.append(jnp.dot(jnp.concatenate(lhs, axis=0), bd2,
                                    preferred_element_type=F32))
            for grp, res in zip(groups, res2):
                for c_i, (bb, q) in enumerate(grp):
                    for i, p in enumerate((2 * q, 2 * q + 1)):
                        acc = jnp.zeros((SUBLANES, LANES), F32)
                        for g in range(A_HD // SUBLANES):
                            r0 = c_i * A_HD + g * SUBLANES
                            acc = jnp.where(pick[g], res[r0:r0 + SUBLANES, lanes_of(i)], acc)
                        y_row = jnp.sum(acc, axis=0, keepdims=True)
                        ytile[bb, p] = jnp.where(row8 == j, y_row, ytile[bb, p])
        for (bb, p), tile in ytile.items():
            y_s[bb, rows, lanes_of(p)] = tile
        return carry

    lax.fori_loop(0, lt // SUBLANES, steps, 0)

    inv_hd = 1.0 / A_HD
    for bb in range(bb_n):
        for p in range(N_PAIR):
            sl = lanes_of(p)
            y = y_s[bb, :, sl]
            d = y - _seg_sum(y, bd) * inv_hd
            var = _seg_sum(d * d, bd) * inv_hd
            yn = d * lax.rsqrt(var + LN_EPS) * g_ref[:, sl] + b_ref[:, sl]
            bonus = _seg_sum(r_s[bb, :, sl] * k_s[bb, :, sl] * rk_ref[:, sl], bd) * v_s[bb, :, sl]
            y_ref[bb, :, sl] = ((yn + bonus) * z_ref[bb, :, sl]).astype(BF16)

    @pl.when(t == pl.num_programs(1) - 1)
    def _():
        for bb in range(bb_n):
            for p in range(N_PAIR):
                s = s_scr[bb, p]
                sout_ref[bb, 2 * p] = s[:, :A_HD]
                sout_ref[bb, 2 * p + 1] = s[:, A_HD:]


def _rwkv(pa, z, shift, s0, layer, wts):
    bn, ln, _ = pa.shape
    lt = _pick_tile(ln, (128,))
    bb_n = _pick_tile(bn, (4, 2))
    row = lambda n: _full((1, n))
    tok = lambda n, j: pl.BlockSpec((bb_n, lt, n), lambda b, t: (b, t, j))
    scr = lambda: pltpu.VMEM((bb_n, lt, BRANCH_W), F32)
    return pl.pallas_call(
        functools.partial(_rwkv_kernel, bb_n=bb_n, lt=lt),
        grid=(bn // bb_n, ln // lt),
        in_specs=[tok(A_COLS, 0),
                  pl.BlockSpec((1, bb_n, 1, A_COLS), lambda b, t: (layer, b, 0, 0)),
                  pl.BlockSpec((1, bb_n, A_HEADS, A_HD, A_HD), lambda b, t: (layer, b, 0, 0, 0)),
                  tok(BRANCH_W, 0),
                  row(A_COLS), row(BRANCH_W), row(BRANCH_W), _full((LANES, 2 * BRANCH_W)),
                  row(BRANCH_W), row(BRANCH_W), row(BRANCH_W), row(BRANCH_W), row(BRANCH_W)],
        out_specs=[tok(BRANCH_W, 0),
                   pl.BlockSpec((bb_n, A_HEADS, A_HD, A_HD), lambda b, t: (b, 0, 0, 0))],
        out_shape=[jax.ShapeDtypeStruct((bn, ln, BRANCH_W), BF16),
                   jax.ShapeDtypeStruct((bn, A_HEADS, A_HD, A_HD), F32)],
        scratch_shapes=[pltpu.VMEM((bb_n, N_PAIR, A_HD, LANES), F32),
                        pltpu.VMEM((bb_n, N_PAIR, A_HD, LANES), BF16),
                        pltpu.VMEM((bb_n, 1, A_COLS), F32),
                        scr(), scr(), scr(), scr(), scr(), scr(), scr()],
        compiler_params=_params("parallel", "arbitrary"),
        name="rwkv7",
    )(pa, shift, s0, z, wts["a_mu"], wts["a_w0"], wts["a_a0"], wts["a_lora"], wts["a_k_k"],
      wts["a_k_a"], wts["a_r_k"], wts["a_ln_g"], wts["a_ln_b"])


def _rope_swap(x, half):
    w = x.shape[-1]
    if 2 * half == w:
        return pltpu.roll(x, half, axis=1)
    lane = lax.broadcasted_iota(jnp.int32, x.shape, 1)
    return jnp.where(lane % (2 * half) < half,
                     pltpu.roll(x, w - half, axis=1), pltpu.roll(x, half, axis=1))


def _swa_kernel(sink_ref, q_ref, k_ref, v_ref, z_ref, cs_ref, sn_ref, kb_ref, vb_ref,
                y_ref, ko_ref, pk_scr, pv_scr, *, bb_n, lb, nb, hs, pos0):
    n = pl.program_id(1)

    @pl.when(n == 0)
    def _():
        pk_scr[...] = kb_ref[0]
        pv_scr[...] = vb_ref[0]

    cs = cs_ref[...]
    sn = sn_ref[...]
    cs_q = jnp.concatenate([cs] * (B_HEADS * B_HD // LANES), axis=1)
    sn_q = jnp.concatenate([sn] * (B_HEADS * B_HD // LANES), axis=1)
    rows = hs * lb
    r_idx = lax.broadcasted_iota(jnp.int32, (rows, WINDOW + lb), 0)
    a_idx = lax.rem(r_idx, lb)
    c_idx = lax.broadcasted_iota(jnp.int32, (rows, WINDOW + lb), 1)
    key_pos = pos0 + n * lb - WINDOW + c_idx
    mask = (c_idx >= a_idx) & (c_idx <= WINDOW + a_idx) & (key_pos >= 0)
    blk = lax.broadcasted_iota(jnp.int32, (rows, 1), 0) // lb
    sinks = []
    for h0 in range(0, B_HEADS, hs):
        col = jnp.full((rows, 1), sink_ref[h0], F32)
        for i in range(1, hs):
            col = jnp.where(blk == i, sink_ref[h0 + i], col)
        sinks.append(col)
    group = B_HEADS // B_KV_HEADS
    ones_kv = jnp.ones((WINDOW + lb, B_HD), BF16)

    def per_b(bb, carry):
        q = q_ref[bb]
        q = q * cs_q + _rope_swap(q, B_HD // 2) * sn_q
        k = k_ref[bb]
        k = k * cs + _rope_swap(k, B_HD // 2) * sn
        v = v_ref[bb]
        ko_ref[bb] = k
        keys = jnp.concatenate([pk_scr[bb], k], axis=0).astype(BF16)
        vals = jnp.concatenate([pv_scr[bb], v], axis=0).astype(BF16)
        if nb > 1:
            pk_scr[bb] = k
            pv_scr[bb] = v
        scores = []
        for h0 in range(0, B_HEADS, hs):
            g = h0 // group
            qs = jnp.concatenate([q[:, h * B_HD:(h + 1) * B_HD] for h in range(h0, h0 + hs)],
                                 axis=0).astype(BF16)
            scores.append(lax.dot_general(qs, keys[:, g * B_HD:(g + 1) * B_HD], NT_DIMS,
                                          preferred_element_type=F32))
        probs, tails = [], []
        for s, sink in zip(scores, sinks):
            s = jnp.where(mask, s * (B_HD ** -0.5), NEG_INF)
            m = jnp.maximum(jnp.max(s, axis=-1, keepdims=True), sink)
            probs.append(jnp.exp(s - m).astype(BF16))
            tails.append(jnp.exp(sink - m))
        outs = []
        for i, (p, tail) in enumerate(zip(probs, tails)):
            g = (i * hs) // group
            pv = jnp.dot(p, vals[:, g * B_HD:(g + 1) * B_HD], preferred_element_type=F32)
            psum = jnp.dot(p, ones_kv, preferred_element_type=F32)
            o = pv / (psum + tail)
            outs.extend(o[j * lb:(j + 1) * lb] for j in range(hs))
        for j in range(B_HEADS // 2):
            sl = slice(j * LANES, (j + 1) * LANES)
            o = jnp.concatenate([outs[2 * j], outs[2 * j + 1]], axis=1)
            y_ref[bb, :, sl] = (o * z_ref[bb, :, sl]).astype(BF16)
        return carry

    lax.fori_loop(0, bb_n, per_b, 0, unroll=min(bb_n, 4))


def _swa(pb, z, k_buf, v_buf, layer, sinks, cos_t, sin_t, pos0):
    bn, ln, _ = pb.shape
    lb = WINDOW if ln % WINDOW == 0 else ln
    nb = ln // lb
    bb_n = 1 if nb > 1 else _pick_tile(bn, (8,))
    group = B_HEADS // B_KV_HEADS
    hs = group if group * lb <= LANES else 1
    qw = B_HEADS * B_HD
    tok = lambda n, j: pl.BlockSpec((bb_n, lb, n), lambda b, t: (b, t, j))
    buf = pl.BlockSpec((1, bb_n, WINDOW, LANES), lambda b, t: (layer, b, 0, 0))
    tab = pl.BlockSpec((lb, LANES), lambda b, t: (t, 0))
    return pl.pallas_call(
        functools.partial(_swa_kernel, bb_n=bb_n, lb=lb, nb=nb, hs=hs, pos0=pos0),
        grid=(bn // bb_n, nb),
        in_specs=[pl.BlockSpec(memory_space=pltpu.SMEM),
                  tok(qw, 0), tok(LANES, qw // LANES), tok(LANES, qw // LANES + 1),
                  tok(BRANCH_W, 1), tab, tab, buf, buf],
        out_specs=[tok(BRANCH_W, 0), tok(LANES, 0)],
        out_shape=[jax.ShapeDtypeStruct((bn, ln, BRANCH_W), BF16),
                   jax.ShapeDtypeStruct((bn, ln, LANES), F32)],
        scratch_shapes=[pltpu.VMEM((bb_n, WINDOW, LANES), F32),
                        pltpu.VMEM((bb_n, WINDOW, LANES), F32)],
        compiler_params=_params("parallel", "arbitrary"),
        name="swa",
    )(sinks, pb, pb, pb, z, cos_t, sin_t, k_buf, v_buf)


def _split3(x):
    hi = x.astype(BF16)
    rem = x - hi.astype(F32)
    mid = rem.astype(BF16)
    return hi, mid, (rem - mid.astype(F32)).astype(BF16)


def _dot_sel(x, sel, dims=(((1,), (0,)), ((), ()))):
    return sum(lax.dot_general(p, sel, dims, preferred_element_type=F32) for p in _split3(x))


def _cols_to_rows(x, n_rows):
    r = lax.broadcasted_iota(jnp.int32, (n_rows, LANES), 0)
    c = lax.broadcasted_iota(jnp.int32, (n_rows, LANES), 1)
    sel = jnp.where(r == c, 1.0, 0.0).astype(BF16)
    return sum(lax.dot_general(sel, p, NT_DIMS, preferred_element_type=F32) for p in _split3(x))


def _mlstm_kernel(qk_ref, v_ref, gt_ref, z_ref, cb_ref, cw_ref, cbias_ref, gb_ref,
                  c0_ref, n0_ref, n0h_ref, m0_ref, g_ref, b_ref, eli_ref, eb_ref,
                  y_ref, cout_ref, nout_ref, mout_ref,
                  c_scr, n_scr, nb_scr, m_scr, prev_scr, act_scr, gl_scr, *, bb_n, lt, lc):
    t = pl.program_id(1)

    @pl.when(t == 0)
    def _():
        c_scr[...] = c0_ref[0]
        n_scr[...] = n0_ref[0]
        m_scr[...] = m0_ref[0]
        prev_scr[...] = cb_ref[0]
        for bb in range(bb_n):
            nb = _dot_sel(n0h_ref[0, bb], eli_ref[:C_HEADS, :], TN_DIMS)
            for h in range(C_HEADS):
                nb_scr[bb, h] = nb[:, h * LANES:(h + 1) * LANES]

    row8 = lax.broadcasted_iota(jnp.int32, (SUBLANES, C_CONV_COLS), 0)
    lane = lax.broadcasted_iota(jnp.int32, (lt, LANES), 1)
    qi = lax.broadcasted_iota(jnp.int32, (lc, lc), 0)
    ki = lax.broadcasted_iota(jnp.int32, (lc, lc), 1)
    tril = ki <= qi
    tril_bf = jnp.where(tril, 1.0, 0.0).astype(BF16)
    ones_bf = jnp.ones((lc, LANES), BF16)
    qkw = C_HEADS * C_QK

    def per_b(bb, carry):
        u = qk_ref[bb]
        prev8 = prev_scr[bb]
        conv = cbias_ref[...] + u * cw_ref[C_CONV - 1:C_CONV, :]
        for s in range(1, C_CONV):
            sh = pltpu.roll(u, s, axis=0)
            head = jnp.where(row8 < s, pltpu.roll(prev8, s, axis=0), sh[:SUBLANES])
            sh = head if lt == SUBLANES else jnp.concatenate([head, sh[SUBLANES:]], axis=0)
            conv = conv + sh * cw_ref[C_CONV - 1 - s:C_CONV - s, :]
        act_scr[...] = conv * jax.nn.sigmoid(conv)
        prev_scr[bb] = u[lt - SUBLANES:, :]
        g = gt_ref[bb] + gb_ref[...]
        gl_scr[...] = jnp.where(lane < C_HEADS, g, -_softplus(-g))

        def chunk(r0):
            glc = gl_scr[pl.ds(r0, lc), :]
            cum = sum(jnp.dot(tril_bf, p, preferred_element_type=F32) for p in _split3(glc))
            glr = _cols_to_rows(glc, C_HEADS)
            cur = _cols_to_rows(cum, 2 * C_HEADS)
            li_x = _dot_sel(glc, eli_ref[...])
            b_x = _dot_sel(cum, eb_ref[...])
            actc = act_scr[pl.ds(r0, lc), :]
            vc = v_ref[bb, pl.ds(r0, lc), :]
            zc = z_ref[bb, pl.ds(r0, lc), :]
            heads = range(C_HEADS)
            vsl = lambda h: slice(h * C_V, (h + 1) * C_V)
            qbs, vs, qk, qc, qn, kws, gate = [], [], [], [], [], [], []
            for h in heads:
                q = actc[:, h * C_QK:(h + 1) * C_QK]
                k = actc[:, qkw + h * C_QK:qkw + (h + 1) * C_QK] * (C_QK ** -0.5)
                qb = q.astype(BF16)
                vs.append(vc[:, vsl(h)].astype(BF16))
                qk.append(lax.dot_general(qb, k.astype(BF16), NT_DIMS,
                                          preferred_element_type=F32))
                qc.append(jnp.dot(qb, c_scr[bb, h].astype(BF16), preferred_element_type=F32))
                qn.append(jnp.dot(qb, nb_scr[bb, h].astype(BF16), preferred_element_type=F32))
                b_q = b_x[:, vsl(h)]
                li_row = glr[h:h + 1, :]
                b_row = cur[C_HEADS + h:C_HEADS + h + 1, :]
                m_s = m_scr[bb, h]
                dmat = jnp.where(tril, b_q[:, :lc] - b_row + li_row, NEG_INF)
                inter = b_q + m_s
                mt = jnp.maximum(inter, jnp.max(dmat, axis=1, keepdims=True))
                b_last = b_row[:, lc - 1:lc]
                gl_row = b_last - b_row + li_row
                gl_col = b_last - b_q[:, :C_QK] + li_x[:, h * LANES:h * LANES + C_QK]
                m_new = jnp.maximum(b_last + m_s[:, :1], jnp.max(gl_row, axis=1, keepdims=True))
                kws.append(k * jnp.exp(gl_col - m_new))
                gate.append((jnp.exp(dmat - mt[:, :lc]), jnp.exp(inter - mt), mt, m_new,
                             jnp.exp(b_last + m_s[:, :1] - m_new)))
            nums, dens = [], []
            for h in heads:
                wts, sc, _, _, _ = gate[h]
                a = (qk[h] * wts).astype(BF16)
                nums.append(jnp.dot(a, vs[h], preferred_element_type=F32) + sc * qc[h])
                dens.append(jnp.dot(a, ones_bf, preferred_element_type=F32) + sc * qn[h])
            for h in heads:
                _, _, _, m_new, s_old = gate[h]
                upd = lax.dot_general(kws[h].astype(BF16),
                                      jnp.concatenate([vs[h], ones_bf], axis=1), TN_DIMS,
                                      preferred_element_type=F32)
                c_scr[bb, h] = s_old * c_scr[bb, h] + upd[:, :C_V]
                nb_scr[bb, h] = s_old * nb_scr[bb, h] + upd[:, C_V:]
                n_scr[bb, h] = s_old * n_scr[bb, h] + jnp.sum(kws[h], axis=0, keepdims=True)
                m_scr[bb, h] = jnp.broadcast_to(m_new, (1, LANES))
            for h in heads:
                hh = nums[h] / jnp.maximum(jnp.abs(dens[h]), jnp.exp(-gate[h][2]))
                yn = _ln_lanes(hh, g_ref[:, vsl(h)], b_ref[:, vsl(h)])
                y_ref[bb, pl.ds(r0, lc), vsl(h)] = (yn * zc[:, vsl(h)]).astype(BF16)

        if lt == lc:
            chunk(0)
        else:
            def body(c, cc):
                chunk(pl.multiple_of(c * lc, lc))
                return cc
            lax.fori_loop(0, lt // lc, body, 0)
        return carry

    lax.fori_loop(0, bb_n, per_b, 0, unroll=min(bb_n, 4))

    @pl.when(t == pl.num_programs(1) - 1)
    def _():
        cout_ref[...] = c_scr[...]
        nout_ref[...] = n_scr[...]
        mout_ref[...] = m_scr[...]


def _mlstm(qk, v, gates, z, conv8, c0, n0, m0, layer, wts):
    bn, ln, _ = qk.shape
    lc = CHUNK if ln % CHUNK == 0 else ln
    lt = _pick_tile(ln, (256,))
    bb_n = 1 if ln > lt or bn < SUBLANES else SUBLANES
    tok = lambda n, j: pl.BlockSpec((bb_n, lt, n), lambda b, t: (b, t, j))
    st_in = lambda *s: pl.BlockSpec((1, bb_n) + s, lambda b, t: (layer, b) + (0,) * len(s))
    st_out = lambda *s: pl.BlockSpec((bb_n,) + s, lambda b, t: (b,) + (0,) * len(s))
    row = lambda n: _full((1, n))
    col = lax.broadcasted_iota(jnp.int32, (LANES, C_HEADS * LANES), 0)
    blk = lax.broadcasted_iota(jnp.int32, (LANES, C_HEADS * LANES), 1) // LANES
    e_li = (col == blk).astype(BF16)
    e_b = (col == blk + C_HEADS).astype(BF16)
    return pl.pallas_call(
        functools.partial(_mlstm_kernel, bb_n=bb_n, lt=lt, lc=lc),
        grid=(bn // bb_n, ln // lt),
        in_specs=[tok(C_CONV_COLS, 0), tok(BRANCH_W, 0), tok(LANES, 0), tok(BRANCH_W, 2),
                  st_in(SUBLANES, C_CONV_COLS), _full((C_CONV, C_CONV_COLS)), row(C_CONV_COLS),
                  row(LANES),
                  st_in(C_HEADS, C_QK, C_V), st_in(C_HEADS, 1, C_QK), st_in(C_HEADS, C_QK),
                  st_in(C_HEADS, 1, LANES),
                  row(BRANCH_W), row(BRANCH_W),
                  _full((LANES, C_HEADS * LANES)), _full((LANES, C_HEADS * LANES))],
        out_specs=[tok(BRANCH_W, 0), st_out(C_HEADS, C_QK, C_V), st_out(C_HEADS, 1, C_QK),
                   st_out(C_HEADS, 1, LANES)],
        out_shape=[jax.ShapeDtypeStruct((bn, ln, BRANCH_W), BF16),
                   jax.ShapeDtypeStruct((bn, C_HEADS, C_QK, C_V), F32),
                   jax.ShapeDtypeStruct((bn, C_HEADS, 1, C_QK), F32),
                   jax.ShapeDtypeStruct((bn, C_HEADS, 1, LANES), F32)],
        scratch_shapes=[pltpu.VMEM((bb_n, C_HEADS, C_QK, C_V), F32),
                        pltpu.VMEM((bb_n, C_HEADS, 1, C_QK), F32),
                        pltpu.VMEM((bb_n, C_HEADS, C_QK, LANES), F32),
                        pltpu.VMEM((bb_n, C_HEADS, 1, LANES), F32),
                        pltpu.VMEM((bb_n, SUBLANES, C_CONV_COLS), F32),
                        pltpu.VMEM((lt, C_CONV_COLS), F32),
                        pltpu.VMEM((lt, LANES), F32)],
        compiler_params=_params("parallel", "arbitrary"),
        name="mlstm",
    )(qk, v, gates, z, conv8, wts["c_conv_w"], wts["c_conv_b"], wts["c_gate_b"],
      c0, n0, n0[:, :, :, 0, :], m0, wts["c_ln_g"], wts["c_ln_b"], e_li, e_b)


def _ret_kernel(q_ref, k_ref, v_ref, z_ref, cs_ref, sn_ref, dm_ref, qd_ref, kd_ref, cd_ref,
                s0_ref, g_ref, b_ref, y_ref, sout_ref, s_scr, *, bb_n, lt, lc):
    t = pl.program_id(1)

    @pl.when(t == 0)
    def _():
        s_scr[...] = s0_ref[0]

    heads = range(D_HEADS)
    lanes_of = lambda h: slice(h * LANES, (h + 1) * LANES)

    def per_b(bb, carry):
        def chunk(r0):
            rows = pl.ds(r0, lc)
            cs = cs_ref[rows, :]
            sn = sn_ref[rows, :]
            qs, ks, vs, inner, cross = [], [], [], [], []
            for h in heads:
                q = q_ref[bb, rows, lanes_of(h)]
                k = k_ref[bb, rows, lanes_of(h)]
                q = (q * cs + _rope_swap(q, D_QK // 2) * sn).astype(BF16)
                k = (k * cs + _rope_swap(k, D_QK // 2) * sn) * (D_QK ** -0.5)
                qs.append(q)
                ks.append(k)
                vs.append(v_ref[bb, rows, lanes_of(h)].astype(BF16))
                inner.append(lax.dot_general(q, k.astype(BF16), NT_DIMS,
                                             preferred_element_type=F32))
                cross.append(jnp.dot(q, s_scr[bb, h].astype(BF16), preferred_element_type=F32))
            outs = []
            for h in heads:
                a = (inner[h] * dm_ref[h]).astype(BF16)
                outs.append(jnp.dot(a, vs[h], preferred_element_type=F32) + cross[h] * qd_ref[h])
            for h in heads:
                s_scr[bb, h] = cd_ref[h] * s_scr[bb, h] + lax.dot_general(
                    (ks[h] * kd_ref[h]).astype(BF16), vs[h], TN_DIMS, preferred_element_type=F32)
            for h in heads:
                yn = _ln_lanes(outs[h], g_ref[:, lanes_of(h)], b_ref[:, lanes_of(h)])
                y_ref[bb, rows, lanes_of(h)] = (yn * z_ref[bb, rows, lanes_of(h)]).astype(BF16)

        if lt == lc:
            chunk(0)
        else:
            def body(c, cc):
                chunk(pl.multiple_of(c * lc, lc))
                return cc
            lax.fori_loop(0, lt // lc, body, 0)
        return carry

    lax.fori_loop(0, bb_n, per_b, 0, unroll=min(bb_n, 4))

    @pl.when(t == pl.num_programs(1) - 1)
    def _():
        sout_ref[...] = s_scr[...]


def _retention(pd, z, s0, layer, wts, cos_t, sin_t):
    bn, ln, _ = pd.shape
    lc = CHUNK if ln % CHUNK == 0 else ln
    lt = _pick_tile(ln, (256,))
    bb_n = 1 if ln > lt or bn < SUBLANES else SUBLANES
    hw = D_HEADS * D_QK
    tok = lambda j: pl.BlockSpec((bb_n, lt, hw), lambda b, t: (b, t, j))
    tab = pl.BlockSpec((lt, LANES), lambda b, t: (t, 0))
    dec = wts["d_tables"][lc]
    return pl.pallas_call(
        functools.partial(_ret_kernel, bb_n=bb_n, lt=lt, lc=lc),
        grid=(bn // bb_n, ln // lt),
        in_specs=[tok(0), tok(1), tok(2), tok(3), tab, tab,
                  _full((D_HEADS, lc, lc)), _full((D_HEADS, lc, LANES)),
                  _full((D_HEADS, lc, LANES)), _full((D_HEADS, 1, LANES)),
                  pl.BlockSpec((1, bb_n, D_HEADS, D_QK, D_V), lambda b, t: (layer, b, 0, 0, 0)),
                  _full((1, BRANCH_W)), _full((1, BRANCH_W))],
        out_specs=[tok(0),
                   pl.BlockSpec((bb_n, D_HEADS, D_QK, D_V), lambda b, t: (b, 0, 0, 0))],
        out_shape=[jax.ShapeDtypeStruct((bn, ln, BRANCH_W), BF16),
                   jax.ShapeDtypeStruct((bn, D_HEADS, D_QK, D_V), F32)],
        scratch_shapes=[pltpu.VMEM((bb_n, D_HEADS, D_QK, D_V), F32)],
        compiler_params=_params("parallel", "arbitrary"),
        name="retention",
    )(pd, pd, pd, z, cos_t, sin_t, dec["decay_mat"], dec["q_dec"], dec["k_dec"], dec["c_dec"],
      s0, wts["d_ln_g"], wts["d_ln_b"])


def _rope_tables(pos, d):
    inv = ROPE_THETA ** (-jnp.arange(0, d, 2, dtype=F32) / d)
    ang = pos.astype(F32)[:, None] * inv[None, :]
    cos = jnp.cos(ang)
    sin = jnp.sin(ang)
    reps = LANES // d
    return (jnp.tile(jnp.concatenate([cos, cos], -1), (1, reps)),
            jnp.tile(jnp.concatenate([-sin, sin], -1), (1, reps)))


def _retention_tables(lc):
    log_gamma = jnp.log1p(-jnp.exp2(-5.0 - jnp.arange(D_HEADS, dtype=F32)))
    idx = jnp.arange(lc, dtype=F32)
    rel = idx[:, None] - idx[None, :]
    decay_mat = jnp.where(rel >= 0, jnp.exp(jnp.maximum(rel, 0.0) * log_gamma[:, None, None]), 0.0)
    q_dec = jnp.exp((idx + 1.0)[None, :] * log_gamma[:, None])
    k_dec = jnp.exp((lc - 1.0 - idx)[None, :] * log_gamma[:, None])
    c_dec = jnp.exp(lc * log_gamma)
    bcast = lambda a: jnp.broadcast_to(a[..., None], a.shape + (LANES,))
    return {"decay_mat": decay_mat, "q_dec": bcast(q_dec), "k_dec": bcast(k_dec),
            "c_dec": bcast(c_dec[:, None])}


def _layer_weights(l, w_in, a_mu, a_w0, a_w_up, a_a0, a_a_up, a_k_k, a_k_a, a_r_k, a_ln_g, a_ln_b,
                   b_sinks, c_conv_w, c_conv_b, c_i_bias, c_f_bias, c_ln_g, c_ln_b,
                   d_ln_g, d_ln_b, w_branch, w_out, ln_g, ln_b, chunk_lens):
    o1 = A_COLS
    o2 = o1 + B_COLS
    o3 = o2 + C_COLS
    o4 = MIX_COLS
    o5 = o4 + N_BRANCH * BRANCH_W
    og = o2 + C_CONV_COLS + C_HEADS * C_V
    wl = w_in[l]
    seg = lambda a, b: wl[:, a:b].astype(BF16)
    row = lambda a: a[l].reshape(1, -1)
    zeros = jnp.zeros((A_DECAY_LORA, BRANCH_W), F32)
    lora = jnp.concatenate([jnp.concatenate([a_w_up[l], zeros], 1),
                            jnp.concatenate([zeros, a_a_up[l]], 1)], 0).astype(BF16)
    gate_b = jnp.pad(jnp.concatenate([c_i_bias[l], c_f_bias[l]]), (0, LANES - 2 * C_HEADS))
    return {
        "w_a": seg(0, o1), "w_b": seg(o1, o2), "w_cqk": seg(o2, o2 + C_CONV_COLS),
        "w_cv": seg(o2 + C_CONV_COLS, og),
        "w_cg": jnp.pad(wl[:, og:o3], ((0, 0), (0, LANES - 2 * C_HEADS))).astype(BF16),
        "w_d": seg(o3, o4), "w_z": seg(o4, o5), "w_g": seg(o5, wl.shape[1]),
        "a_mu": row(a_mu), "a_w0": row(a_w0), "a_a0": row(a_a0), "a_lora": lora,
        "a_k_k": row(a_k_k), "a_k_a": row(a_k_a), "a_r_k": row(a_r_k),
        "a_ln_g": row(a_ln_g), "a_ln_b": row(a_ln_b),
        "b_sinks": b_sinks[l],
        "c_conv_w": c_conv_w[l], "c_conv_b": row(c_conv_b), "c_gate_b": gate_b.reshape(1, LANES),
        "c_ln_g": row(c_ln_g), "c_ln_b": row(c_ln_b),
        "d_ln_g": row(d_ln_g), "d_ln_b": row(d_ln_b),
        "d_tables": {lc: _retention_tables(lc) for lc in chunk_lens},
        "w_branch": w_branch[l].astype(BF16), "w_out": w_out[l].astype(BF16),
        "ln_g": row(ln_g), "ln_b": row(ln_b),
    }


def _hybrid_layer(xf, xb, pos0, l, st, wts, tabs):
    bn, ln, _ = xf.shape
    m = bn * ln
    s_a, shift_a, k_buf, v_buf, c_m, n_m, m_m, conv_m, s_d = st
    tok = lambda a: a.reshape(bn, ln, a.shape[-1])
    pa = tok(_matmul(xb, wts["w_a"], name="proj_a"))
    pb = tok(_matmul(xb, wts["w_b"], name="proj_b"))
    pcqk = tok(_matmul(xb, wts["w_cqk"], name="proj_cqk"))
    pcv = tok(_matmul(xb, wts["w_cv"], name="proj_cv"))
    pcg = tok(_matmul(xb, wts["w_cg"], name="proj_cg"))
    pd = tok(_matmul(xb, wts["w_d"], name="proj_d"))
    z = tok(_matmul(xb, wts["w_z"], act="silu", name="proj_z"))
    gate = _matmul(xb, wts["w_g"], act="sigmoid", name="proj_gate")

    y_a, s_a_new = _rwkv(pa, z, shift_a, s_a, l, wts)
    shift_new = pa[:, -1]

    y_b, k_rot = _swa(pb, z, k_buf, v_buf, l, wts["b_sinks"], tabs["cos64"], tabs["sin64"], pos0)
    qw = B_HEADS * B_HD
    kvw = B_KV_HEADS * B_HD
    kv_shape = (bn, WINDOW, B_KV_HEADS, B_HD)
    k_new = jnp.concatenate([k_buf[l], k_rot], 1)[:, -WINDOW:].reshape(kv_shape)
    v_new = jnp.concatenate([v_buf[l], pb[..., qw + kvw:]], 1)[:, -WINDOW:].reshape(kv_shape)

    y_c, c_new, n_new, m_new = _mlstm(pcqk, pcv, pcg, z, conv_m, c_m, n_m, m_m, l, wts)
    conv_new = jnp.concatenate([conv_m[l, :, SUBLANES - (C_CONV - 1):], pcqk], 1)[:, -(C_CONV - 1):]

    y_d, s_d_new = _retention(pd, z, s_d, l, wts, tabs["cos128"], tabs["sin128"])

    flat = lambda a: a.reshape(m, a.shape[-1])
    merged = _merge([flat(y_a), flat(y_b), flat(y_c), flat(y_d)], gate, wts["w_branch"])
    xf_new, xb_new = _out_ln(merged, wts["w_out"], flat(xf), wts["ln_g"], wts["ln_b"])
    new = (s_a_new, shift_new, k_new, v_new, c_new, n_new[:, :, 0], m_new[:, :, 0, 0],
           conv_new, s_d_new)
    return xf_new.reshape(bn, ln, D_MODEL), xb_new, new


def _trunk(x, pos0, states, layer_wts):
    bn, ln, _ = x.shape
    s_a, shift_a, k_buf, v_buf, c_m, n_m, m_m, conv_m, s_d = states
    depth = s_a.shape[0]
    st = (s_a,
          shift_a[:, :, None, :],
          k_buf.reshape(depth, bn, WINDOW, LANES),
          v_buf.reshape(depth, bn, WINDOW, LANES),
          c_m,
          n_m[:, :, :, None, :],
          jnp.broadcast_to(m_m[..., None, None], m_m.shape + (1, LANES)),
          jnp.pad(conv_m, ((0, 0), (0, 0), (SUBLANES - (C_CONV - 1), 0), (0, 0))),
          s_d)
    pos = pos0 + jnp.arange(ln)
    cos64, sin64 = _rope_tables(pos, B_HD)
    cos128, sin128 = _rope_tables(pos, D_QK)
    tabs = {"cos64": cos64, "sin64": sin64, "cos128": cos128, "sin128": sin128}
    xb = x.reshape(bn * ln, D_MODEL).astype(BF16)
    new = []
    for l in range(depth):
        x, xb, st_new = _hybrid_layer(x, xb, pos0, l, st, layer_wts[l], tabs)
        new.append(st_new)
    stacked = tuple(jnp.stack([s[j] for s in new]) for j in range(len(states)))
    return x, stacked


def kernel(x_prompt, x_sample, state_rwkv_S, state_rwkv_shift, cache_swa_k, cache_swa_v,
           state_mlstm_C, state_mlstm_n, state_mlstm_m, state_mlstm_conv, state_ret_S,
           w_in, a_mu, a_w0, a_w_up, a_a0, a_a_up, a_k_k, a_k_a, a_r_k, a_ln_g, a_ln_b,
           b_sinks, c_conv_w, c_conv_b, c_i_bias, c_f_bias, c_ln_g, c_ln_b,
           d_ln_g, d_ln_b, w_branch, w_out, ln_g, ln_b):
    weights = (w_in, a_mu, a_w0, a_w_up, a_a0, a_a_up, a_k_k, a_k_a, a_r_k, a_ln_g, a_ln_b,
               b_sinks, c_conv_w, c_conv_b, c_i_bias, c_f_bias, c_ln_g, c_ln_b,
               d_ln_g, d_ln_b, w_branch, w_out, ln_g, ln_b)
    sample_states = (state_rwkv_S, state_rwkv_shift, cache_swa_k, cache_swa_v,
                     state_mlstm_C, state_mlstm_n, state_mlstm_m, state_mlstm_conv, state_ret_S)
    depth = w_in.shape[0]
    chunk_len = lambda ln: CHUNK if ln % CHUNK == 0 else ln
    chunk_lens = {chunk_len(x_prompt.shape[1]), chunk_len(x_sample.shape[1])}
    layer_wts = [_layer_weights(l, *weights, chunk_lens) for l in range(depth)]
    n_prompt = x_prompt.shape[0]
    zero_states = tuple(jnp.zeros((depth, n_prompt) + s.shape[2:], s.dtype) for s in sample_states)
    y_prompt, p = _trunk(x_prompt, 0, zero_states, layer_wts)
    y_sample, s = _trunk(x_sample, PAST_LEN, sample_states, layer_wts)
    return (y_prompt, y_sample, p[0], s[0], p[1], s[1], p[2], s[2], p[3], s[3], p[4], s[4],
            p[5], s[5], p[6], s[6], p[7], s[7], p[8], s[8])
```
